```python
import jax, jax.numpy as jnp
from jax import lax
import numpy as np

D_MODEL = 2048
BATCH = 4
SEQ = 2048
DEPTH = 2
DEC_BATCH = 128
DEC_SEQ = 1
PAST_LEN = 2048
PAGE_SIZE = 128

HEAD_DIM = 128
FOX_HEADS = 8
FOX_WIDTH = FOX_HEADS * HEAD_DIM
NSA_HEADS = 8
NSA_KV_HEADS = 2
NSA_HPG = NSA_HEADS // NSA_KV_HEADS
NSA_WIDTH = NSA_HEADS * HEAD_DIM
CMP_BLOCK = 64
SEL_BLOCK = 64
TOP_N = 16
N_FORCED_LOCAL = 2
WINDOW = 512
Q_BLOCK = 128
RMS_EPS = 1e-6
NEG = -1e30
SCALE = HEAD_DIM ** -0.5
SPLIT_SIZES = (FOX_WIDTH, 2 * FOX_WIDTH, FOX_HEADS, FOX_WIDTH,
               NSA_WIDTH, 4 * NSA_KV_HEADS * HEAD_DIM, 2 * NSA_KV_HEADS * HEAD_DIM, 3 * NSA_HEADS, NSA_WIDTH,
               2 * D_MODEL)
N_IN = sum(SPLIT_SIZES)

kernel_name = "fox_nsa_gated_parallel_decode_step"


def rmsnorm(x, g):
    x32 = x.astype(jnp.float32)
    y = x32 * lax.rsqrt(jnp.mean(x32 * x32, axis=-1, keepdims=True) + RMS_EPS)
    return (y * g.astype(jnp.float32)).astype(x.dtype)


def masked_softmax(s, mask):
    s = jnp.where(mask, s, NEG)
    m = jnp.max(s, axis=-1, keepdims=True)
    e = jnp.where(mask, jnp.exp(s - m), 0.0)
    return e / jnp.maximum(jnp.sum(e, axis=-1, keepdims=True), 1e-30)


def alibi_slopes(n):
    start = 2.0 ** (-8.0 / n)
    return jnp.asarray([start ** (i + 1) for i in range(n)], dtype=jnp.float32)


def unblock(o):
    o = jnp.moveaxis(o, 0, 1)
    return o.reshape((o.shape[0], -1) + o.shape[3:])


def project(x, g, w_in, b_f):
    B, T, _ = x.shape
    h = rmsnorm(x, g)
    proj = h @ w_in
    pts = [int(v) for v in np.cumsum(SPLIT_SIZES)[:-1]]
    fq, fkv, ff, fgate, nq, nkv, nwkv, nbg, ngate, mg = jnp.split(proj, pts, axis=-1)
    fq = fq.reshape(B, T, FOX_HEADS, HEAD_DIM)
    fkv = fkv.reshape(B, T, 2, FOX_HEADS, HEAD_DIM)
    logf = jax.nn.log_sigmoid((ff + b_f).astype(jnp.float32))
    nq = nq.reshape(B, T, NSA_KV_HEADS, NSA_HPG, HEAD_DIM)
    nkv = nkv.reshape(B, T, 4, NSA_KV_HEADS, HEAD_DIM)
    nwkv = nwkv.reshape(B, T, 2, NSA_KV_HEADS, HEAD_DIM)
    nbg = jax.nn.sigmoid(nbg.astype(jnp.float32)).reshape(B, T, 3, NSA_KV_HEADS, NSA_HPG)
    mg = jax.nn.sigmoid(mg.astype(jnp.float32)).reshape(B, T, 2, D_MODEL)
    return fq, fkv, logf, fgate, nq, nkv, nwkv, nbg, ngate, mg


def fox_attention(q, k, v, logf, q_start):
    B, Tq, H, hd = q.shape
    L = k.shape[1]
    c = jnp.cumsum(logf.astype(jnp.float32), axis=1).transpose(0, 2, 1)
    k_pos = jnp.arange(L)
    qb = min(Q_BLOCK, Tq)
    nb = Tq // qb

    def blk(i):
        start = q_start + i * qb
        qi = lax.dynamic_slice_in_dim(q, i * qb, qb, axis=1)
        ci = lax.dynamic_slice_in_dim(c, start, qb, axis=2)
        q_pos = start + jnp.arange(qb)
        s = jnp.einsum('bqhd,bkhd->bhqk', qi, k, preferred_element_type=jnp.float32) * SCALE
        s = s + ci[..., None] - c[:, :, None, :]
        p = masked_softmax(s, k_pos[None, :] <= q_pos[:, None])
        return jnp.einsum('bhqk,bkhd->bqhd', p.astype(v.dtype), v)

    return unblock(lax.map(blk, jnp.arange(nb)))


def compress(rows, w1, w2, pos_emb):
    B, L, G, hd = rows.shape
    n = L // CMP_BLOCK
    blocks = rows[:, :n * CMP_BLOCK].reshape(B, n, CMP_BLOCK, G, hd) + pos_emb[:, None, :]
    blocks = blocks.transpose(0, 1, 3, 2, 4).reshape(B, n, G, CMP_BLOCK * hd)
    return jax.nn.silu(blocks @ w1) @ w2


def nsa_cmp_slc(q, kc, vc, ks, vs, q_start, w1, w2, pos_emb, slopes):
    B, Tq, G, J, hd = q.shape
    L = kc.shape[1]
    k_cmp = compress(kc, w1[0], w2[0], pos_emb[0])
    v_cmp = compress(vc, w1[1], w2[1], pos_emb[1])
    n_cmp = k_cmp.shape[1]
    cmp_end = (jnp.arange(n_cmp) + 1) * CMP_BLOCK - 1
    n_sel = -(-L // SEL_BLOCK)
    top = min(TOP_N, n_sel)
    pad = n_sel * SEL_BLOCK - L
    ks_b = jnp.pad(ks, ((0, 0), (0, pad), (0, 0), (0, 0))).reshape(B, n_sel, SEL_BLOCK, G, hd).transpose(0, 3, 1, 2, 4)
    vs_b = jnp.pad(vs, ((0, 0), (0, pad), (0, 0), (0, 0))).reshape(B, n_sel, SEL_BLOCK, G, hd).transpose(0, 3, 1, 2, 4)
    blk_ids = jnp.arange(n_sel)
    in_blk = jnp.arange(SEL_BLOCK)
    gather = jax.vmap(jax.vmap(lambda blocks, ix: blocks[ix]))
    qb = min(Q_BLOCK, Tq)
    nb = Tq // qb

    def blk(i):
        start = q_start + i * qb
        qi = lax.dynamic_slice_in_dim(q, i * qb, qb, axis=1)
        q_pos = start + jnp.arange(qb)
        s = jnp.einsum('bqgjd,bngd->bgjqn', qi, k_cmp, preferred_element_type=jnp.float32) * SCALE
        dist_c = q_pos[:, None] - cmp_end[None, :]
        s = s - slopes[None, :, :, None, None] * dist_c
        p = masked_softmax(s, dist_c >= 0)
        o_cmp = jnp.einsum('bgjqn,bngd->bqgjd', p.astype(v_cmp.dtype), v_cmp)
        imp = jnp.pad(p.sum(axis=2), ((0, 0), (0, 0), (0, 0), (0, n_sel - n_cmp)))
        own = (q_pos // SEL_BLOCK)[:, None]
        forced = ((blk_ids <= own) & (blk_ids > own - N_FORCED_LOCAL)) | (blk_ids == 0)
        imp = jnp.where(forced, 1e9, jnp.where(blk_ids > own, -1e9, imp))
        _, idx = lax.top_k(imp, top)
        kg = gather(ks_b, idx)
        vg = gather(vs_b, idx)
        tok_pos = idx[..., None] * SEL_BLOCK + in_blk
        dist = q_pos[None, None, :, None, None] - tok_pos
        s2 = jnp.einsum('bqgjd,bgqnpd->bgjqnp', qi, kg, preferred_element_type=jnp.float32) * SCALE
        s2 = s2 - slopes[None, :, :, None, None, None] * dist[:, :, None]
        s2 = s2.reshape(B, G, J, qb, top * SEL_BLOCK)
        p2 = masked_softmax(s2, (dist >= 0).reshape(B, G, 1, qb, top * SEL_BLOCK))
        o_slc = jnp.einsum('bgjqm,bgqmd->bqgjd', p2.astype(vg.dtype), vg.reshape(B, G, qb, top * SEL_BLOCK, hd))
        return o_cmp, o_slc

    o_cmp, o_slc = lax.map(blk, jnp.arange(nb))
    return unblock(o_cmp), unblock(o_slc)


def window_attend(q, k, v, q_pos, k_pos, slopes):
    s = jnp.einsum('bqgjd,bkgd->bgjqk', q, k, preferred_element_type=jnp.float32) * SCALE
    dist = q_pos[:, None] - k_pos[None, :]
    s = s - slopes[None, :, :, None, None] * dist
    mask = (dist >= 0) & (dist < WINDOW) & (k_pos[None, :] >= 0)
    p = masked_softmax(s, mask)
    return jnp.einsum('bgjqk,bkgd->bqgjd', p.astype(v.dtype), v)


def window_prompt(q, k, v, slopes):
    T = q.shape[1]
    qb = min(Q_BLOCK, T)
    nb = T // qb
    kp = jnp.pad(k, ((0, 0), (WINDOW, 0), (0, 0), (0, 0)))
    vp = jnp.pad(v, ((0, 0), (WINDOW, 0), (0, 0), (0, 0)))

    def blk(i):
        qi = lax.dynamic_slice_in_dim(q, i * qb, qb, axis=1)
        ki = lax.dynamic_slice_in_dim(kp, i * qb, WINDOW + qb, axis=1)
        vi = lax.dynamic_slice_in_dim(vp, i * qb, WINDOW + qb, axis=1)
        q_pos = i * qb + jnp.arange(qb)
        k_pos = i * qb - WINDOW + jnp.arange(WINDOW + qb)
        return window_attend(qi, ki, vi, q_pos, k_pos, slopes)

    return unblock(lax.map(blk, jnp.arange(nb)))


def combine_nsa(nbg, o_cmp, o_slc, o_win):
    o = nbg[:, :, 0, :, :, None] * o_cmp + nbg[:, :, 1, :, :, None] * o_slc + nbg[:, :, 2, :, :, None] * o_win
    return o.astype(o_cmp.dtype)


def merge(x, o_fox, fgate, o_nsa, ngate, mg, w_bf, w_bn, w_o):
    B, T, _ = x.shape
    ya = (o_fox.reshape(B, T, FOX_WIDTH) * jax.nn.silu(fgate)) @ w_bf
    yb = (o_nsa.reshape(B, T, NSA_WIDTH) * jax.nn.silu(ngate)) @ w_bn
    m = (mg[:, :, 0] * ya + mg[:, :, 1] * yb).astype(x.dtype)
    return x + m @ w_o


def setup_inputs(seed: int = 0) -> dict:
    key = jax.random.key(seed)
    ks = jax.random.split(key, 17)
    n_pages = PAST_LEN // PAGE_SIZE
    n_used = DEC_BATCH * n_pages
    n_phys = n_used + (n_used + 3) // 4
    win_buf = min(WINDOW, PAST_LEN)
    nrm = jax.random.normal
    return {
        "x_prompt": nrm(ks[0], (BATCH, SEQ, D_MODEL), jnp.float32),
        "x_sample": nrm(ks[1], (DEC_BATCH, DEC_SEQ, D_MODEL), jnp.float32),
        "cache_fox_kv": nrm(ks[2], (DEPTH, n_phys, PAGE_SIZE, 2, FOX_HEADS, HEAD_DIM), jnp.float32),
        "cache_fox_logf": jax.nn.log_sigmoid(4.0 + nrm(ks[3], (DEPTH, n_phys, PAGE_SIZE, FOX_HEADS), jnp.float32)),
        "cache_nsa_kv": nrm(ks[4], (DEPTH, n_phys, PAGE_SIZE, 4, NSA_KV_HEADS, HEAD_DIM), jnp.float32),
        "cache_nsa_win": nrm(ks[5], (DEPTH, DEC_BATCH, win_buf, 2, NSA_KV_HEADS, HEAD_DIM), jnp.float32),
        "page_table": jax.random.permutation(ks[6], n_phys)[:n_used].reshape(DEC_BATCH, n_pages).astype(jnp.int32),
        "norm_g": 1.0 + 0.05 * nrm(ks[7], (DEPTH, D_MODEL), jnp.float32),
        "w_in": nrm(ks[8], (DEPTH, D_MODEL, N_IN), jnp.float32) * D_MODEL ** -0.5,
        "b_fox_f": 4.0 + 0.5 * nrm(ks[9], (DEPTH, FOX_HEADS), jnp.float32),
        "w_cmp1": nrm(ks[10], (DEPTH, 2, CMP_BLOCK * HEAD_DIM, HEAD_DIM), jnp.float32) * (CMP_BLOCK * HEAD_DIM) ** -0.5,
        "w_cmp2": nrm(ks[11], (DEPTH, 2, HEAD_DIM, HEAD_DIM), jnp.float32) * HEAD_DIM ** -0.5,
        "cmp_pos": 0.1 * nrm(ks[12], (DEPTH, 2, CMP_BLOCK, HEAD_DIM), jnp.float32),
        "w_branch_fox": nrm(ks[13], (DEPTH, FOX_WIDTH, D_MODEL), jnp.float32) * FOX_WIDTH ** -0.5,
        "w_branch_nsa": nrm(ks[14], (DEPTH, NSA_WIDTH, D_MODEL), jnp.float32) * NSA_WIDTH ** -0.5,
        "w_out": nrm(ks[15], (DEPTH, D_MODEL, D_MODEL), jnp.float32) * D_MODEL ** -0.5,
        "final_norm_g": 1.0 + 0.05 * nrm(ks[16], (D_MODEL,), jnp.float32),
    }


def reference(x_prompt, x_sample, cache_fox_kv, cache_fox_logf, cache_nsa_kv, cache_nsa_win, page_table,
              norm_g, w_in, b_fox_f, w_cmp1, w_cmp2, cmp_pos, w_branch_fox, w_branch_nsa, w_out, final_norm_g):
    slopes = alibi_slopes(NSA_HEADS).reshape(NSA_KV_HEADS, NSA_HPG)
    n_dec, n_pages = page_table.shape
    past = n_pages * cache_fox_kv.shape[2]
    win_buf = cache_nsa_win.shape[2]
    seq = x_prompt.shape[1]
    dec_seq = x_sample.shape[1]
    win_p = min(WINDOW, seq)
    xp, xs = x_prompt, x_sample
    fkv_p, logf_p, nkv_p, win_p_l = [], [], [], []
    fkv_s, logf_s, nkv_s, win_s_l = [], [], [], []
    for l in range(DEPTH):
        fq, fkv, logf, fgate, nq, nkv, nwkv, nbg, ngate, mg = project(xp, norm_g[l], w_in[l], b_fox_f[l])
        o_fox = fox_attention(fq, fkv[:, :, 0], fkv[:, :, 1], logf, 0)
        o_cmp, o_slc = nsa_cmp_slc(nq, nkv[:, :, 0], nkv[:, :, 1], nkv[:, :, 2], nkv[:, :, 3], 0,
                                   w_cmp1[l], w_cmp2[l], cmp_pos[l], slopes)
        o_win = window_prompt(nq, nwkv[:, :, 0], nwkv[:, :, 1], slopes)
        xp = merge(xp, o_fox, fgate, combine_nsa(nbg, o_cmp, o_slc, o_win), ngate, mg,
                   w_branch_fox[l], w_branch_nsa[l], w_out[l])
        fkv_p.append(fkv)
        logf_p.append(logf)
        nkv_p.append(nkv)
        win_p_l.append(nwkv[:, seq - win_p:])
        fq, fkv, logf, fgate, nq, nkv, nwkv, nbg, ngate, mg = project(xs, norm_g[l], w_in[l], b_fox_f[l])
        past_fkv = cache_fox_kv[l, page_table].reshape((n_dec, past) + cache_fox_kv.shape[3:])
        past_logf = cache_fox_logf[l, page_table].reshape((n_dec, past) + cache_fox_logf.shape[3:])
        past_nkv = cache_nsa_kv[l, page_table].reshape((n_dec, past) + cache_nsa_kv.shape[3:])
        fkv_all = jnp.concatenate([past_fkv, fkv], axis=1)
        logf_all = jnp.concatenate([past_logf, logf], axis=1)
        nkv_all = jnp.concatenate([past_nkv, nkv], axis=1)
        win_all = jnp.concatenate([cache_nsa_win[l], nwkv], axis=1)
        o_fox = fox_attention(fq, fkv_all[:, :, 0], fkv_all[:, :, 1], logf_all, past)
        o_cmp, o_slc = nsa_cmp_slc(nq, nkv_all[:, :, 0], nkv_all[:, :, 1], nkv_all[:, :, 2], nkv_all[:, :, 3], past,
                                   w_cmp1[l], w_cmp2[l], cmp_pos[l], slopes)
        q_pos = past + jnp.arange(dec_seq)
        k_pos = past - win_buf + jnp.arange(win_buf + dec_seq)
        o_win = window_attend(nq, win_all[:, :, 0], win_all[:, :, 1], q_pos, k_pos, slopes)
        xs = merge(xs, o_fox, fgate, combine_nsa(nbg, o_cmp, o_slc, o_win), ngate, mg,
                   w_branch_fox[l], w_branch_nsa[l], w_out[l])
        fkv_s.append(fkv)
        logf_s.append(logf)
        nkv_s.append(nkv)
        win_s_l.append(win_all[:, win_all.shape[1] - win_buf:])
    y_prompt = rmsnorm(xp, final_norm_g)
    y_sample = rmsnorm(xs, final_norm_g)
    new_fox_kv_prompt = jnp.stack(fkv_p, axis=0)
    new_fox_logf_prompt = jnp.stack(logf_p, axis=0)
    new_nsa_kv_prompt = jnp.stack(nkv_p, axis=0)
    new_nsa_win_prompt = jnp.stack(win_p_l, axis=0)
    new_fox_kv_sample = jnp.stack(fkv_s, axis=0)
    new_fox_logf_sample = jnp.stack(logf_s, axis=0)
    new_nsa_kv_sample = jnp.stack(nkv_s, axis=0)
    new_nsa_win_sample = jnp.stack(win_s_l, axis=0)
    return (y_prompt, y_sample, new_fox_kv_prompt, new_fox_logf_prompt, new_nsa_kv_prompt, new_nsa_win_prompt,
            new_fox_kv_sample, new_fox_logf_sample, new_nsa_kv_sample, new_nsa_win_sample)
```

```python
import functools

import numpy as np
import jax
import jax.numpy as jnp
from jax import lax
from jax.experimental import pallas as pl
from jax.experimental.pallas import tpu as pltpu

F32 = jnp.float32
BF16 = jnp.bfloat16
HIGHEST = lax.Precision.HIGHEST

D_MODEL = 2048
HEAD_DIM = 128
FOX_HEADS = 8
FOX_WIDTH = FOX_HEADS * HEAD_DIM
NSA_HEADS = 8
NSA_KV_HEADS = 2
NSA_HPG = NSA_HEADS // NSA_KV_HEADS
NSA_WIDTH = NSA_HEADS * HEAD_DIM
CMP_BLOCK = 64
SEL_BLOCK = 64
TOP_N = 16
N_FORCED_LOCAL = 2
WINDOW = 512
RMS_EPS = 1e-6
NEG = -1e30
SCALE = HEAD_DIM ** -0.5
SPLIT_SIZES = (FOX_WIDTH, 2 * FOX_WIDTH, FOX_HEADS, FOX_WIDTH,
               NSA_WIDTH, 4 * NSA_KV_HEADS * HEAD_DIM, 2 * NSA_KV_HEADS * HEAD_DIM, 3 * NSA_HEADS, NSA_WIDTH,
               2 * D_MODEL)
SMALL_W = 128
GATE_COL = FOX_HEADS
ALIBI_START = 2.0 ** (-8.0 / NSA_HEADS)

VMEM_LIMIT = 56 * 1024 * 1024

NT_DIMS = (((1,), (1,)), ((), ()))
TN_DIMS = (((0,), (0,)), ((), ()))


def _params(*sem):
    return pltpu.CompilerParams(dimension_semantics=sem, vmem_limit_bytes=VMEM_LIMIT)


def _sigmoid(x):
    return 1.0 / (1.0 + jnp.exp(-x))


def _iota(shape, dim):
    return lax.broadcasted_iota(jnp.int32, shape, dim)


def _div_pow2(x, n):
    assert n & (n - 1) == 0
    return x >> (n.bit_length() - 1)


def _rmsnorm_body(x_ref, g_ref, o_ref):
    x = x_ref[...]
    ms = jnp.mean(x * x, axis=-1, keepdims=True)
    o_ref[...] = (x * lax.rsqrt(ms + RMS_EPS) * g_ref[...]).astype(o_ref.dtype)


def _rmsnorm(x2d, g, out_dtype, tm):
    m, d = x2d.shape
    return pl.pallas_call(
        _rmsnorm_body,
        grid=(m // tm,),
        in_specs=[pl.BlockSpec((tm, d), lambda i: (i, 0)), pl.BlockSpec((1, d), lambda i: (0, 0))],
        out_specs=pl.BlockSpec((tm, d), lambda i: (i, 0)),
        out_shape=jax.ShapeDtypeStruct((m, d), out_dtype),
        compiler_params=_params("parallel"),
        name="rmsnorm",
    )(x2d, g.reshape(1, d))


def _mm_body(h_ref, w_ref, o_ref, *, act):
    acc = jnp.dot(h_ref[...], w_ref[...], preferred_element_type=F32)
    if act == "sigmoid":
        acc = _sigmoid(acc)
    o_ref[...] = acc


def _mm(h, w, act=None, tn=512):
    m, k = h.shape
    n = w.shape[1]
    tm = min(m, 1024)
    tn = min(tn, n)
    return pl.pallas_call(
        functools.partial(_mm_body, act=act),
        grid=(m // tm, n // tn),
        in_specs=[pl.BlockSpec((tm, k), lambda i, j: (i, 0)), pl.BlockSpec((k, tn), lambda i, j: (0, j))],
        out_specs=pl.BlockSpec((tm, tn), lambda i, j: (i, j)),
        out_shape=jax.ShapeDtypeStruct((m, n), F32),
        compiler_params=_params("parallel", "arbitrary"),
        name="proj_" + (act or "lin"),
    )(h, w)


def _mm_small_body(h_ref, w_ref, b_ref, o_ref):
    v = jnp.dot(h_ref[...], w_ref[...], preferred_element_type=F32) + b_ref[...]
    col = _iota(v.shape, 1)
    logsig = jnp.minimum(v, 0.0) - jnp.log1p(jnp.exp(-jnp.abs(v)))
    o_ref[...] = jnp.where(col < FOX_HEADS, logsig, _sigmoid(v))


def _mm_small(h, w, b):
    m, k = h.shape
    tm = min(m, 1024)
    return pl.pallas_call(
        _mm_small_body,
        grid=(m // tm,),
        in_specs=[pl.BlockSpec((tm, k), lambda i: (i, 0)), pl.BlockSpec((k, SMALL_W), lambda i: (0, 0)),
                  pl.BlockSpec((1, SMALL_W), lambda i: (0, 0))],
        out_specs=pl.BlockSpec((tm, SMALL_W), lambda i: (i, 0)),
        out_shape=jax.ShapeDtypeStruct((m, SMALL_W), F32),
        compiler_params=_params("parallel"),
        name="proj_small",
    )(h, w, b)


def _mm_slab_body(h_ref, w_ref, o_ref, *, nslab):
    acc = jnp.dot(h_ref[...], w_ref[...], preferred_element_type=F32)
    for c in range(nslab):
        o_ref[c] = acc[:, c * HEAD_DIM:(c + 1) * HEAD_DIM]


def _mm_slab(h, w):
    m, k = h.shape
    n = w.shape[1]
    nslab = n // HEAD_DIM
    tm = min(m, 1024)
    return pl.pallas_call(
        functools.partial(_mm_slab_body, nslab=nslab),
        grid=(m // tm,),
        in_specs=[pl.BlockSpec((tm, k), lambda i: (i, 0)), pl.BlockSpec((k, n), lambda i: (0, 0))],
        out_specs=pl.BlockSpec((nslab, tm, HEAD_DIM), lambda i: (0, i, 0)),
        out_shape=jax.ShapeDtypeStruct((nslab, m, HEAD_DIM), F32),
        compiler_params=_params("parallel"),
        name="proj_slab",
    )(h, w)


def _compress_body(x_ref, pos_ref, w1_ref, w2_ref, o_ref):
    x = x_ref[0] + pos_ref[0]
    h1 = jnp.dot(x.astype(BF16), w1_ref[0], preferred_element_type=F32)
    a = h1 * _sigmoid(h1)
    o_ref[0] = jnp.dot(a.astype(BF16), w2_ref[0], preferred_element_type=F32)


def _compress(xc, pos, w1, w2):
    nslab, r, kk = xc.shape
    tr = min(r, 256)
    kind = lambda c, i: (c // NSA_KV_HEADS, 0, 0)
    return pl.pallas_call(
        _compress_body,
        grid=(nslab, r // tr),
        in_specs=[pl.BlockSpec((1, tr, kk), lambda c, i: (c, i, 0)),
                  pl.BlockSpec((1, 1, kk), kind),
                  pl.BlockSpec((1, kk, HEAD_DIM), kind),
                  pl.BlockSpec((1, HEAD_DIM, HEAD_DIM), kind)],
        out_specs=pl.BlockSpec((1, tr, HEAD_DIM), lambda c, i: (c, i, 0)),
        out_shape=jax.ShapeDtypeStruct((nslab, r, HEAD_DIM), F32),
        compiler_params=_params("parallel", "parallel"),
        name="nsa_compress",
    )(xc, pos, w1, w2)


def _cumsum_body(lf_ref, ccol_ref, crow_ref, *, t):
    tri = (_iota((128, 128), 0) >= _iota((128, 128), 1)).astype(F32)
    carry = jnp.zeros((1, SMALL_W), F32)
    for ch in range(t // 128):
        rows = slice(ch * 128, (ch + 1) * 128)
        cs = jnp.dot(tri, lf_ref[rows, :], precision=HIGHEST, preferred_element_type=F32) + carry
        ccol_ref[rows, :] = cs
        crow_ref[0, :, rows] = cs.T[0:FOX_HEADS, :]
        carry = cs[127:128, :]


def _cumsum_logf(small, b, t):
    return pl.pallas_call(
        functools.partial(_cumsum_body, t=t),
        grid=(b,),
        in_specs=[pl.BlockSpec((t, SMALL_W), lambda i: (i, 0))],
        out_specs=[pl.BlockSpec((t, SMALL_W), lambda i: (i, 0)),
                   pl.BlockSpec((1, FOX_HEADS, t), lambda i: (i, 0, 0))],
        out_shape=[jax.ShapeDtypeStruct((b * t, SMALL_W), F32), jax.ShapeDtypeStruct((b, FOX_HEADS, t), F32)],
        compiler_params=_params("parallel"),
        name="fox_cumsum",
    )(small)


def _flash_loop(q, k_ref, v_ref, kt_lo, kt_hi, tk, score_fn, select, m_sc, l_sc, acc_sc):
    m_sc[...] = jnp.full(m_sc.shape, NEG, F32)
    l_sc[...] = jnp.zeros(l_sc.shape, F32)
    acc_sc[...] = jnp.zeros(acc_sc.shape, F32)

    def body(kt, carry):
        k0 = pl.multiple_of(kt * tk, tk)
        kb = k_ref[pl.ds(k0, tk), :].astype(BF16)
        vb = v_ref[pl.ds(k0, tk), :].astype(BF16)
        s = lax.dot_general(q, kb, NT_DIMS, preferred_element_type=F32)
        s, mask = score_fn(s, kt, k0)
        s = select(mask, s, NEG)
        m_old = m_sc[...]
        m_new = jnp.maximum(m_old, jnp.max(s, axis=-1, keepdims=True))
        alpha = jnp.exp(m_old - m_new)
        p = select(mask, jnp.exp(s - m_new), 0.0)
        l_sc[...] = alpha * l_sc[...] + jnp.sum(p, axis=-1, keepdims=True)
        acc_sc[...] = alpha * acc_sc[...] + jnp.dot(p.astype(BF16), vb, preferred_element_type=F32)
        m_sc[...] = m_new
        return carry

    lax.fori_loop(kt_lo, kt_hi, body, 0)
    return acc_sc[...] / jnp.maximum(l_sc[...], 1e-30)


def _fox_body(q_ref, k_ref, v_ref, ccol_ref, crow_ref, o_ref, m_sc, l_sc, acc_sc, *, tq, tk):
    h = pl.program_id(1)
    q0 = pl.program_id(2) * tq
    q = q_ref[...].astype(BF16)
    lane = _iota((tq, SMALL_W), 1)
    cq = jnp.sum(jnp.where(lane == h, ccol_ref[...], 0.0), axis=-1, keepdims=True)
    qpos = q0 + _iota((tq, 1), 0)
    kloc = _iota((1, tk), 1)

    def score_fn(s, kt, k0):
        ck = crow_ref[0, 0, kt]
        return s * SCALE + cq - ck, (k0 + kloc) <= qpos

    n_kt = (q0 + tq + tk - 1) // tk
    o_ref[...] = _flash_loop(q, k_ref, v_ref, 0, n_kt, tk, score_fn, jnp.where, m_sc, l_sc, acc_sc)


def _fox_prompt(fq, fkv, ccol, crow, b, t):
    tq = tk = 512
    nq, nkt = t // tq, t // tk
    crow5 = crow.reshape(b, FOX_HEADS, nkt, 1, tk)
    return pl.pallas_call(
        functools.partial(_fox_body, tq=tq, tk=tk),
        grid=(b, FOX_HEADS, nq),
        in_specs=[pl.BlockSpec((tq, HEAD_DIM), lambda i, h, j: (i * nq + j, h)),
                  pl.BlockSpec((t, HEAD_DIM), lambda i, h, j: (i, h)),
                  pl.BlockSpec((t, HEAD_DIM), lambda i, h, j: (i, FOX_HEADS + h)),
                  pl.BlockSpec((tq, SMALL_W), lambda i, h, j: (i * nq + j, 0)),
                  pl.BlockSpec((1, 1, nkt, 1, tk), lambda i, h, j: (i, h, 0, 0, 0))],
        out_specs=pl.BlockSpec((tq, HEAD_DIM), lambda i, h, j: (i * nq + j, h)),
        out_shape=jax.ShapeDtypeStruct((b * t, FOX_WIDTH), F32),
        scratch_shapes=[pltpu.VMEM((tq, 1), F32), pltpu.VMEM((tq, 1), F32), pltpu.VMEM((tq, HEAD_DIM), F32)],
        compiler_params=_params("parallel", "parallel", "arbitrary"),
        name="fox_prompt",
    )(fq, fkv, fkv, ccol, crow5)


def _masked_softmax(s, mask, axis):
    s = jnp.where(mask, s, NEG)
    m = jnp.max(s, axis=axis, keepdims=True)
    e = jnp.where(mask, jnp.exp(s - m), 0.0)
    return e / jnp.maximum(jnp.sum(e, axis=axis, keepdims=True), 1e-30)


def _group_slope_scale(g):
    scale = jnp.float32(1.0)
    for gg in range(1, NSA_KV_HEADS):
        scale = jnp.where(g == gg, jnp.float32(ALIBI_START ** (NSA_HPG * gg)), scale)
    return scale


def _head_slopes(head_idx, n):
    out = jnp.zeros(head_idx.shape, F32)
    for hh in range(n):
        out = jnp.where(head_idx == hh, jnp.float32(ALIBI_START ** (hh + 1)), out)
    return out


def _nsa_body(q_ref, kc_ref, vc_ref, ks_ref, vs_ref, kw_ref, vw_ref, ocmp_ref, oslc_ref, owin_ref,
              m_sc, l_sc, acc_sc, sel_sc, *, tq, tk, n_blk):
    g = pl.program_id(1)
    q0 = pl.program_id(2) * tq
    rows = NSA_HPG * tq
    lg_tq = tq.bit_length() - 1
    qf = q_ref[...]
    q4 = jnp.concatenate([qf[:, j * HEAD_DIM:(j + 1) * HEAD_DIM] for j in range(NSA_HPG)], axis=0).astype(BF16)
    gscale = _group_slope_scale(g)

    def unstack(o):
        return jnp.concatenate([o[j * tq:(j + 1) * tq] for j in range(NSA_HPG)], axis=1)

    ri = _iota((rows, 1), 0)
    qpos = q0 + (ri & (tq - 1))
    slope = _head_slopes(ri >> lg_tq, NSA_HPG) * gscale
    kc = kc_ref[0].astype(BF16)
    vc = vc_ref[0].astype(BF16)
    cend = (_iota((1, n_blk), 1) + 1) * CMP_BLOCK - 1
    dc = qpos - cend
    sc = lax.dot_general(q4, kc, NT_DIMS, preferred_element_type=F32) * SCALE - slope * dc.astype(F32)
    p = _masked_softmax(sc, dc >= 0, -1)
    ocmp_ref[...] = unstack(jnp.dot(p.astype(BF16), vc, preferred_element_type=F32))

    ci = _iota((1, rows), 1)
    qpos_t = q0 + (ci & (tq - 1))
    slope_t = _head_slopes(ci >> lg_tq, NSA_HPG) * gscale
    blk = _iota((n_blk, 1), 0)
    dct = qpos_t - ((blk + 1) * CMP_BLOCK - 1)
    sct = lax.dot_general(kc, q4, NT_DIMS, preferred_element_type=F32) * SCALE - slope_t * dct.astype(F32)
    pt = _masked_softmax(sct, dct >= 0, 0)
    imp = pt[:, 0:tq]
    for j in range(1, NSA_HPG):
        imp = imp + pt[:, j * tq:(j + 1) * tq]
    own = _div_pow2(q0 + _iota((1, tq), 1), SEL_BLOCK)
    forced = ((blk <= own) & (blk > own - N_FORCED_LOCAL)) | (blk == 0)
    imp = jnp.where(forced, 1e9, jnp.where(blk > own, -1e9, imp))
    rank = jnp.zeros((n_blk, tq), F32)
    for mm in range(n_blk):
        row = imp[mm:mm + 1, :]
        rank = rank + jnp.where(blk > mm, (row >= imp).astype(F32), (row > imp).astype(F32))
    sel_sc[...] = (rank < TOP_N).astype(F32)

    kloc = _iota((1, tk), 1)
    qpos_q = q0 + _iota((tq, 1), 0)
    bias0 = slope * (qpos - kloc).astype(F32)
    e_row = _iota((n_blk, tk), 0)
    e_blk = _div_pow2(_iota((n_blk, tk), 1), SEL_BLOCK)

    def alibi(s, k0):
        return s * SCALE - bias0 + slope * k0.astype(F32)

    def select(mask_q, x, fill):
        return jnp.where(mask_q[None], x.reshape(NSA_HPG, tq, tk), fill).reshape(rows, tk)

    def slc_score(s, kt, k0):
        expand = (e_row == kt * (tk // SEL_BLOCK) + e_blk).astype(BF16)
        picked = lax.dot_general(sel_sc[...].astype(BF16), expand, TN_DIMS, preferred_element_type=F32)
        mask_q = (picked > 0.5) & ((k0 + kloc) <= qpos_q)
        return alibi(s, k0), mask_q

    def win_score(s, kt, k0):
        dist = qpos_q - (k0 + kloc)
        mask_q = (dist >= 0) & (dist < WINDOW)
        return alibi(s, k0), mask_q

    kt_hi = q0 // tk + 1
    oslc_ref[...] = unstack(_flash_loop(q4, ks_ref, vs_ref, 0, kt_hi, tk, slc_score, select, m_sc, l_sc, acc_sc))
    kt_lo = jnp.maximum(q0 - (WINDOW - 1), 0) // tk
    owin_ref[...] = unstack(_flash_loop(q4, kw_ref, vw_ref, kt_lo, kt_hi, tk, win_score, select, m_sc, l_sc, acc_sc))


def _nsa_prompt(nq, cmpkv, nkv, nwkv, b, t):
    tq, tk = 128, 512
    nqt = t // tq
    n_blk = t // CMP_BLOCK
    gw = NSA_HPG * HEAD_DIM
    rows = NSA_HPG * tq
    qmap = lambda i, g, j: (i * nqt + j, g)
    col = lambda c: (lambda i, g, j: (i, c * NSA_KV_HEADS + g))
    slab = lambda c: (lambda i, g, j: (c * NSA_KV_HEADS + g, i, 0))
    out = jax.ShapeDtypeStruct((b * t, NSA_WIDTH), F32)
    return pl.pallas_call(
        functools.partial(_nsa_body, tq=tq, tk=tk, n_blk=n_blk),
        grid=(b, NSA_KV_HEADS, nqt),
        in_specs=[pl.BlockSpec((tq, gw), qmap),
                  pl.BlockSpec((1, n_blk, HEAD_DIM), slab(0)),
                  pl.BlockSpec((1, n_blk, HEAD_DIM), slab(1)),
                  pl.BlockSpec((t, HEAD_DIM), col(2)),
                  pl.BlockSpec((t, HEAD_DIM), col(3)),
                  pl.BlockSpec((t, HEAD_DIM), col(0)),
                  pl.BlockSpec((t, HEAD_DIM), col(1))],
        out_specs=[pl.BlockSpec((tq, gw), qmap)] * 3,
        out_shape=[out, out, out],
        scratch_shapes=[pltpu.VMEM((rows, 1), F32), pltpu.VMEM((rows, 1), F32), pltpu.VMEM((rows, HEAD_DIM), F32),
                        pltpu.VMEM((n_blk, tq), F32)],
        compiler_params=_params("parallel", "parallel", "arbitrary"),
        name="nsa_prompt",
    )(nq, cmpkv, cmpkv, nkv, nkv, nwkv, nwkv)


def _merge_body(x_ref, of_ref, fg_ref, oc_ref, os_ref, ow_ref, ng_ref, sm_ref, mg_ref, wbf_ref, wbn_ref, wo_ref,
                gf_ref, o_ref, *, final):
    fg = fg_ref[...]
    ya = jnp.dot((of_ref[...] * (fg * _sigmoid(fg))).astype(BF16), wbf_ref[...], preferred_element_type=F32)
    sm = sm_ref[...]
    parts = []
    for hd in range(NSA_HEADS):
        cols = slice(hd * HEAD_DIM, (hd + 1) * HEAD_DIM)
        gate = lambda br: sm[:, GATE_COL + br * NSA_HEADS + hd:GATE_COL + br * NSA_HEADS + hd + 1]
        parts.append(gate(0) * oc_ref[:, cols] + gate(1) * os_ref[:, cols] + gate(2) * ow_ref[:, cols])
    ng = ng_ref[...]
    on = jnp.concatenate(parts, axis=1) * (ng * _sigmoid(ng))
    yb = jnp.dot(on.astype(BF16), wbn_ref[...], preferred_element_type=F32)
    mix = mg_ref[:, 0:D_MODEL] * ya + mg_ref[:, D_MODEL:2 * D_MODEL] * yb
    xn = x_ref[...] + jnp.dot(mix.astype(BF16), wo_ref[...], preferred_element_type=F32)
    if final:
        ms = jnp.mean(xn * xn, axis=-1, keepdims=True)
        xn = xn * lax.rsqrt(ms + RMS_EPS) * gf_ref[...]
    o_ref[...] = xn


def _merge(x, o_fox, fgate, o_cmp, o_slc, o_win, ngate, small, mg, w_bf, w_bn, w_o, g_final, final):
    m = x.shape[0]
    tm = min(m, 128)
    row = lambda w: pl.BlockSpec((tm, w), lambda i: (i, 0))
    whole = lambda a: pl.BlockSpec(a.shape, lambda i: (0, 0), pipeline_mode=pl.Buffered(1))
    gf = g_final.reshape(1, D_MODEL)
    return pl.pallas_call(
        functools.partial(_merge_body, final=final),
        grid=(m // tm,),
        in_specs=[row(D_MODEL), row(FOX_WIDTH), row(FOX_WIDTH), row(NSA_WIDTH), row(NSA_WIDTH), row(NSA_WIDTH),
                  row(NSA_WIDTH), row(SMALL_W), row(2 * D_MODEL), whole(w_bf), whole(w_bn), whole(w_o), whole(gf)],
        out_specs=row(D_MODEL),
        out_shape=jax.ShapeDtypeStruct((m, D_MODEL), F32),
        compiler_params=_params("parallel"),
        name="merge_final" if final else "merge",
    )(x, o_fox, fgate, o_cmp, o_slc, o_win, ngate, small, mg, w_bf, w_bn, w_o, gf)


def _page_specs(layer, n_pages, block, col_block):
    return [pl.BlockSpec((None, None) + block, lambda s, pt, j=j: (layer, pt[s, j], 0, col_block))
            for j in range(n_pages)]


def _gather_cmp_body(pt_ref, *refs, n_pages, page):
    o_ref = refs[n_pages]
    nslab = o_ref.shape[0]
    for j in range(n_pages):
        pg = refs[j][...]
        for c in range(nslab):
            o_ref[c, j * page:(j + 1) * page, :] = pg[:, c * HEAD_DIM:(c + 1) * HEAD_DIM]


def _gather_cmp(cache_nsa4, page_table, layer):
    s, n_pages = page_table.shape
    page = cache_nsa4.shape[2]
    nslab = 2 * NSA_KV_HEADS
    past = n_pages * page
    return pl.pallas_call(
        functools.partial(_gather_cmp_body, n_pages=n_pages, page=page),
        grid_spec=pltpu.PrefetchScalarGridSpec(
            num_scalar_prefetch=1,
            grid=(s,),
            in_specs=_page_specs(layer, n_pages, (page, nslab * HEAD_DIM), 0),
            out_specs=pl.BlockSpec((nslab, past, HEAD_DIM), lambda i, pt: (0, i, 0)),
        ),
        out_shape=jax.ShapeDtypeStruct((nslab, s * past, HEAD_DIM), F32),
        compiler_params=_params("arbitrary"),
        name="dec_gather_cmp",
    )(page_table, *([cache_nsa4] * n_pages))


def _row_block(a, r, width):
    return a[:, r * width:(r + 1) * width]


def _fox_dec_body(pt_ref, *refs, n_pages, page):
    kv = refs[:n_pages]
    lf = refs[n_pages:2 * n_pages]
    q_ref, kvn_ref, smn_ref, o_ref = refs[2 * n_pages:]
    q = q_ref[...]
    sub = _iota((FOX_HEADS, FOX_WIDTH), 0)
    diag = _div_pow2(_iota((FOX_HEADS, FOX_WIDTH), 1), HEAD_DIM) == sub
    q_bd = jnp.where(diag, jnp.broadcast_to(q, (FOX_HEADS, FOX_WIDTH)), 0.0)
    q_bdb = q_bd.astype(BF16)
    kvn = kvn_ref[...]
    k_new = kvn[:, 0:FOX_WIDTH]
    v_new = kvn[:, FOX_WIDTH:2 * FOX_WIDTH]
    s_new = jnp.sum(q_bd * k_new, axis=-1, keepdims=True) * SCALE
    eye = _iota((FOX_HEADS, SMALL_W), 0) == _iota((FOX_HEADS, SMALL_W), 1)
    lf_new = jnp.sum(jnp.where(eye, jnp.broadcast_to(smn_ref[...], (FOX_HEADS, SMALL_W)), 0.0), axis=-1, keepdims=True)
    later = (_iota((page, page), 0) > _iota((page, page), 1)).astype(F32)
    carry = lf_new
    s_pages = [None] * n_pages
    for j in reversed(range(n_pages)):
        lt = lf[j][...].T
        decay = jnp.dot(lt, later, precision=HIGHEST, preferred_element_type=F32) + carry
        carry = carry + jnp.sum(lt, axis=-1, keepdims=True)
        kb = kv[j][:, 0:FOX_WIDTH].astype(BF16)
        s_pages[j] = lax.dot_general(q_bdb, kb, NT_DIMS, preferred_element_type=F32) * SCALE + decay
    s = jnp.concatenate(s_pages, axis=1)
    m = jnp.maximum(jnp.max(s, axis=-1, keepdims=True), s_new)
    e = jnp.exp(s - m)
    e_new = jnp.exp(s_new - m)
    denom = jnp.maximum(jnp.sum(e, axis=-1, keepdims=True) + e_new, 1e-30)
    p = (e / denom).astype(BF16)
    acc = (e_new / denom) * jnp.broadcast_to(v_new, (FOX_HEADS, FOX_WIDTH))
    for j in range(n_pages):
        vb = kv[j][:, FOX_WIDTH:2 * FOX_WIDTH].astype(BF16)
        acc = acc + jnp.dot(p[:, j * page:(j + 1) * page], vb, preferred_element_type=F32)
    o_ref[...] = jnp.sum(jnp.where(diag, acc, 0.0), axis=0, keepdims=True)


def _fox_decode(cache_kv4, cache_lf, page_table, layer, fq, fkv, small):
    s, n_pages = page_table.shape
    page = cache_kv4.shape[2]
    row = lambda w: pl.BlockSpec((None, 1, w), lambda i, pt: (i, 0, 0))
    lf_specs = [pl.BlockSpec((None, None, page, FOX_HEADS), lambda i, pt, j=j: (layer, pt[i, j], 0, 0))
                for j in range(n_pages)]
    out = pl.pallas_call(
        functools.partial(_fox_dec_body, n_pages=n_pages, page=page),
        grid_spec=pltpu.PrefetchScalarGridSpec(
            num_scalar_prefetch=1,
            grid=(s,),
            in_specs=_page_specs(layer, n_pages, (page, 2 * FOX_WIDTH), 0) + lf_specs
            + [row(FOX_WIDTH), row(2 * FOX_WIDTH), row(SMALL_W)],
            out_specs=row(FOX_WIDTH),
        ),
        out_shape=jax.ShapeDtypeStruct((s, 1, FOX_WIDTH), F32),
        compiler_params=_params("arbitrary"),
        name="fox_decode",
    )(page_table, *([cache_kv4] * n_pages), *([cache_lf] * n_pages),
      fq.reshape(s, 1, -1), fkv.reshape(s, 1, -1), small.reshape(s, 1, -1))
    return out.reshape(s, FOX_WIDTH)


def _nsa_dec_body(pt_ref, *refs, n_pages, page, win_buf):
    slc = refs[:n_pages]
    cm_ref, win_ref, q_ref, kvn_ref, wn_ref, ocmp_ref, oslc_ref, owin_ref = refs[n_pages:]
    past = n_pages * page
    n_blk = past // CMP_BLOCK
    own = past // SEL_BLOCK
    gw = NSA_KV_HEADS * HEAD_DIM
    q = q_ref[...]
    sub = _iota((NSA_HEADS, 1), 0)
    grp = _div_pow2(sub, NSA_HPG)
    slope = _head_slopes(sub, NSA_HEADS)
    q_rows = jnp.concatenate([_row_block(q, r, HEAD_DIM) for r in range(NSA_HEADS)], axis=0)
    q_bd = jnp.concatenate([jnp.where(grp == gg, q_rows, 0.0) for gg in range(NSA_KV_HEADS)], axis=1)
    q_bdb = q_bd.astype(BF16)

    def pick_group(full):
        out = full[:, 0:HEAD_DIM]
        for gg in range(1, NSA_KV_HEADS):
            out = jnp.where(grp == gg, full[:, gg * HEAD_DIM:(gg + 1) * HEAD_DIM], out)
        return out

    def as_row(o):
        return jnp.concatenate([o[r:r + 1, :] for r in range(NSA_HEADS)], axis=1)

    kc = jnp.concatenate([cm_ref[gg] for gg in range(NSA_KV_HEADS)], axis=1).astype(BF16)
    vc = jnp.concatenate([cm_ref[NSA_KV_HEADS + gg] for gg in range(NSA_KV_HEADS)], axis=1).astype(BF16)
    nb = _iota((1, n_blk), 1)
    dc = past - ((nb + 1) * CMP_BLOCK - 1)
    sc = lax.dot_general(q_bdb, kc, NT_DIMS, preferred_element_type=F32) * SCALE - slope * dc.astype(F32)
    p = _masked_softmax(sc, dc >= 0, -1)
    ocmp_ref[...] = as_row(pick_group(jnp.dot(p.astype(BF16), vc, preferred_element_type=F32)))

    forced = (nb == 0) | ((nb <= own) & (nb > own - N_FORCED_LOCAL))
    mrow = _iota((n_blk, n_blk), 0)
    ncol = _iota((n_blk, n_blk), 1)
    sel = jnp.zeros((NSA_HEADS, n_blk), F32)
    for gg in range(NSA_KV_HEADS):
        imp = jnp.sum(jnp.where(grp == gg, p, 0.0), axis=0, keepdims=True)
        imp = jnp.where(forced, 1e9, imp)
        imp_b = jnp.broadcast_to(imp, (n_blk, n_blk))
        imp_col = jnp.sum(jnp.where(mrow == ncol, imp_b, 0.0), axis=-1, keepdims=True)
        beats = jnp.where(mrow < ncol, (imp_col >= imp_b).astype(F32), (imp_col > imp_b).astype(F32))
        rank = jnp.sum(beats, axis=0, keepdims=True) + jnp.where(forced, 0.0, 1.0)
        sel = jnp.where(grp == gg, (rank < TOP_N).astype(F32), sel)
    expand = (_iota((n_blk, past), 0) == _div_pow2(_iota((n_blk, past), 1), SEL_BLOCK)).astype(BF16)
    sel_tok = jnp.dot(sel.astype(BF16), expand, preferred_element_type=F32) > 0.5

    kvn = kvn_ref[...]

    def attend(s, mask, s_new, v_tiles, v_new):
        s = jnp.where(mask, s, NEG)
        m = jnp.maximum(jnp.max(s, axis=-1, keepdims=True), s_new)
        e = jnp.where(mask, jnp.exp(s - m), 0.0)
        e_new = jnp.exp(s_new - m)
        denom = jnp.maximum(jnp.sum(e, axis=-1, keepdims=True) + e_new, 1e-30)
        pb = (e / denom).astype(BF16)
        acc = (e_new / denom) * jnp.broadcast_to(v_new, (NSA_HEADS, gw))
        off = 0
        for vt in v_tiles:
            n = vt.shape[0]
            acc = acc + jnp.dot(pb[:, off:off + n], vt.astype(BF16), preferred_element_type=F32)
            off += n
        return as_row(pick_group(acc))

    s_pages = [lax.dot_general(q_bdb, slc[j][:, 0:gw].astype(BF16), NT_DIMS, preferred_element_type=F32)
               for j in range(n_pages)]
    dist = (past - _iota((1, past), 1)).astype(F32)
    s = jnp.concatenate(s_pages, axis=1) * SCALE - slope * dist
    ks_new = kvn[:, 2 * gw:3 * gw]
    vs_new = kvn[:, 3 * gw:4 * gw]
    s_new = jnp.sum(q_bd * ks_new, axis=-1, keepdims=True) * SCALE
    oslc_ref[...] = attend(s, sel_tok, s_new, [slc[j][:, gw:2 * gw] for j in range(n_pages)], vs_new)

    wn = wn_ref[...]
    sw = lax.dot_general(q_bdb, win_ref[:, 0:gw].astype(BF16), NT_DIMS, preferred_element_type=F32)
    idx = _iota((1, win_buf), 1)
    wdist = win_buf - idx
    sw = sw * SCALE - slope * wdist.astype(F32)
    wmask = (wdist < WINDOW) & (past - wdist >= 0)
    sw_new = jnp.sum(q_bd * wn[:, 0:gw], axis=-1, keepdims=True) * SCALE
    owin_ref[...] = attend(sw, wmask, sw_new, [win_ref[:, gw:2 * gw]], wn[:, gw:2 * gw])


def _nsa_decode(cache_nsa4, cache_win4, cmpkv, page_table, layer, nq, nkv, nwkv):
    s, n_pages = page_table.shape
    page = cache_nsa4.shape[2]
    win_buf = cache_win4.shape[2]
    past = n_pages * page
    n_blk = past // CMP_BLOCK
    nslab = 2 * NSA_KV_HEADS
    gw = NSA_KV_HEADS * HEAD_DIM
    row = lambda w: pl.BlockSpec((None, 1, w), lambda i, pt: (i, 0, 0))
    out = jax.ShapeDtypeStruct((s, 1, NSA_WIDTH), F32)
    outs = pl.pallas_call(
        functools.partial(_nsa_dec_body, n_pages=n_pages, page=page, win_buf=win_buf),
        grid_spec=pltpu.PrefetchScalarGridSpec(
            num_scalar_prefetch=1,
            grid=(s,),
            in_specs=_page_specs(layer, n_pages, (page, 2 * gw), 1)
            + [pl.BlockSpec((nslab, n_blk, HEAD_DIM), lambda i, pt: (0, i, 0)),
               pl.BlockSpec((None, None, win_buf, 2 * gw), lambda i, pt: (layer, i, 0, 0)),
               row(NSA_WIDTH), row(2 * nslab * HEAD_DIM), row(2 * gw)],
            out_specs=[row(NSA_WIDTH)] * 3,
        ),
        out_shape=[out, out, out],
        compiler_params=_params("arbitrary"),
        name="nsa_decode",
    )(page_table, *([cache_nsa4] * n_pages), cmpkv, cache_win4,
      nq.reshape(s, 1, -1), nkv.reshape(s, 1, -1), nwkv.reshape(s, 1, -1))
    return [o.reshape(s, NSA_WIDTH) for o in outs]


def _split_weights(w_in_l, b_f_l):
    off = [int(v) for v in np.cumsum((0,) + SPLIT_SIZES)]
    cols = lambda i: w_in_l[:, off[i]:off[i + 1]]
    names = ("fq", "fkv", "ff", "fgate", "nq", "nkv", "nwkv", "nbg", "ngate", "mg")
    w = {n: cols(i) for i, n in enumerate(names)}
    pad = SMALL_W - FOX_HEADS - 3 * NSA_HEADS
    small = jnp.concatenate([w.pop("ff"), w.pop("nbg"), jnp.zeros((D_MODEL, pad), F32)], axis=1)
    out = {n: v.astype(BF16) for n, v in w.items()}
    out["small"] = small.astype(BF16)
    out["b_small"] = jnp.concatenate([b_f_l, jnp.zeros((SMALL_W - FOX_HEADS,), F32)]).reshape(1, SMALL_W)
    out["cmp"] = out["nkv"][:, 0:2 * NSA_KV_HEADS * HEAD_DIM]
    return out


def _project(x2d, g, w):
    h = _rmsnorm(x2d, g, BF16, min(x2d.shape[0], 512))
    pr = {n: _mm(h, w[n]) for n in ("fq", "fkv", "fgate", "nq", "nkv", "nwkv", "ngate")}
    pr["mg"] = _mm(h, w["mg"], act="sigmoid")
    pr["small"] = _mm_small(h, w["small"], w["b_small"])
    return h, pr


def kernel(x_prompt, x_sample, cache_fox_kv, cache_fox_logf, cache_nsa_kv, cache_nsa_win, page_table,
           norm_g, w_in, b_fox_f, w_cmp1, w_cmp2, cmp_pos, w_branch_fox, w_branch_nsa, w_out, final_norm_g):
    b, t, d = x_prompt.shape
    s, dec_seq, _ = x_sample.shape
    depth, n_phys, page = cache_fox_kv.shape[:3]
    n_pages = page_table.shape[1]
    win_buf = cache_nsa_win.shape[2]
    win_p = min(WINDOW, t)
    assert dec_seq == 1 and d == D_MODEL and t % 512 == 0 and (n_pages * page) % SEL_BLOCK == 0
    assert NSA_KV_HEADS == 2 and win_buf <= n_pages * page

    cache_kv4 = cache_fox_kv.reshape(depth, n_phys, page, 2 * FOX_WIDTH)
    cache_nsa4 = cache_nsa_kv.reshape(depth, n_phys, page, 4 * NSA_KV_HEADS * HEAD_DIM)
    cache_win4 = cache_nsa_win.reshape(depth, s, win_buf, 2 * NSA_KV_HEADS * HEAD_DIM)
    page_table = page_table.astype(jnp.int32)

    xp = x_prompt.reshape(b * t, d)
    xs = x_sample.reshape(s, d)
    outs = {k: [] for k in ("fkv_p", "logf_p", "nkv_p", "win_p", "fkv_s", "logf_s", "nkv_s", "win_s")}
    for l in range(depth):
        final = l == depth - 1
        w = _split_weights(w_in[l], b_fox_f[l])
        w_bf = w_branch_fox[l].astype(BF16)
        w_bn = w_branch_nsa[l].astype(BF16)
        w_o = w_out[l].astype(BF16)
        w1 = w_cmp1[l].astype(BF16)
        w2 = w_cmp2[l].astype(BF16)
        pos = cmp_pos[l].reshape(2, 1, CMP_BLOCK * HEAD_DIM)

        h, pr = _project(xp, norm_g[l], w)
        ccol, crow = _cumsum_logf(pr["small"], b, t)
        o_fox = _fox_prompt(pr["fq"], pr["fkv"], ccol, crow, b, t)
        xc = _mm_slab(h, w["cmp"]).reshape(2 * NSA_KV_HEADS, b * (t // CMP_BLOCK), CMP_BLOCK * HEAD_DIM)
        cmpkv = _compress(xc, pos, w1, w2)
        o_cmp, o_slc, o_win = _nsa_prompt(pr["nq"], cmpkv, pr["nkv"], pr["nwkv"], b, t)
        xp = _merge(xp, o_fox, pr["fgate"], o_cmp, o_slc, o_win, pr["ngate"], pr["small"], pr["mg"],
                    w_bf, w_bn, w_o, final_norm_g, final)
        outs["fkv_p"].append(pr["fkv"].reshape(b, t, 2, FOX_HEADS, HEAD_DIM))
        outs["logf_p"].append(pr["small"][:, 0:FOX_HEADS].reshape(b, t, FOX_HEADS))
        outs["nkv_p"].append(pr["nkv"].reshape(b, t, 4, NSA_KV_HEADS, HEAD_DIM))
        outs["win_p"].append(pr["nwkv"].reshape(b, t, 2, NSA_KV_HEADS, HEAD_DIM)[:, t - win_p:])

        _, ps = _project(xs, norm_g[l], w)
        o_fox = _fox_decode(cache_kv4, cache_fox_logf, page_table, l, ps["fq"], ps["fkv"], ps["small"])
        xc = _gather_cmp(cache_nsa4, page_table, l)
        xc = xc.reshape(2 * NSA_KV_HEADS, s * (n_pages * page // CMP_BLOCK), CMP_BLOCK * HEAD_DIM)
        cmpkv = _compress(xc, pos, w1, w2)
        o_cmp, o_slc, o_win = _nsa_decode(cache_nsa4, cache_win4, cmpkv, page_table, l, ps["nq"], ps["nkv"], ps["nwkv"])
        xs = _merge(xs, o_fox, ps["fgate"], o_cmp, o_slc, o_win, ps["ngate"], ps["small"], ps["mg"],
                    w_bf, w_bn, w_o, final_norm_g, final)
        nwkv_s = ps["nwkv"].reshape(s, 1, 2, NSA_KV_HEADS, HEAD_DIM)
        outs["fkv_s"].append(ps["fkv"].reshape(s, 1, 2, FOX_HEADS, HEAD_DIM))
        outs["logf_s"].append(ps["small"][:, 0:FOX_HEADS].reshape(s, 1, FOX_HEADS))
        outs["nkv_s"].append(ps["nkv"].reshape(s, 1, 4, NSA_KV_HEADS, HEAD_DIM))
        outs["win_s"].append(jnp.concatenate([cache_nsa_win[l], nwkv_s], axis=1)[:, 1:])

    st = {k: jnp.stack(v, axis=0) for k, v in outs.items()}
    return (xp.reshape(b, t, d), xs.reshape(s, 1, d), st["fkv_p"], st["logf_p"], st["nkv_p"], st["win_p"],
            st["fkv_s"], st["logf_s"], st["nkv_s"], st["win_s"])
```

```python
import functools

import numpy as np
import jax
import jax.numpy as jnp
from jax import lax
from jax.experimental import pallas as pl
from jax.experimental.pallas import tpu as pltpu

F32 = jnp.float32
BF16 = jnp.bfloat16
HIGHEST = lax.Precision.HIGHEST

D_MODEL = 2048
HEAD_DIM = 128
FOX_HEADS = 8
FOX_WIDTH = FOX_HEADS * HEAD_DIM
NSA_HEADS = 8
NSA_KV_HEADS = 2
NSA_HPG = NSA_HEADS // NSA_KV_HEADS
NSA_WIDTH = NSA_HEADS * HEAD_DIM
CMP_BLOCK = 64
SEL_BLOCK = 64
TOP_N = 16
N_FORCED_LOCAL = 2
WINDOW = 512
RMS_EPS = 1e-6
NEG = -1e30
SCALE = HEAD_DIM ** -0.5
LOG2E = 1.4426950408889634
SPLIT_SIZES = (FOX_WIDTH, 2 * FOX_WIDTH, FOX_HEADS, FOX_WIDTH,
               NSA_WIDTH, 4 * NSA_KV_HEADS * HEAD_DIM, 2 * NSA_KV_HEADS * HEAD_DIM, 3 * NSA_HEADS, NSA_WIDTH,
               2 * D_MODEL)
SMALL_W = 128
GATE_COL = FOX_HEADS
ALIBI_START = 2.0 ** (-8.0 / NSA_HEADS)

VMEM_LIMIT = 56 * 1024 * 1024

NT_DIMS = (((1,), (1,)), ((), ()))
TN_DIMS = (((0,), (0,)), ((), ()))


def _params(*sem):
    return pltpu.CompilerParams(dimension_semantics=sem, vmem_limit_bytes=VMEM_LIMIT)


def _sigmoid(x):
    return 1.0 / (1.0 + jnp.exp(-x))


def _iota(shape, dim):
    return lax.broadcasted_iota(jnp.int32, shape, dim)


def _div_pow2(x, n):
    assert n & (n - 1) == 0
    return x >> (n.bit_length() - 1)


def _rmsnorm_body(x_ref, g_ref, o_ref):
    x = x_ref[...]
    ms = jnp.mean(x * x, axis=-1, keepdims=True)
    o_ref[...] = (x * lax.rsqrt(ms + RMS_EPS) * g_ref[...]).astype(o_ref.dtype)


def _rmsnorm(x2d, g, out_dtype, tm):
    m, d = x2d.shape
    return pl.pallas_call(
        _rmsnorm_body,
        grid=(m // tm,),
        in_specs=[pl.BlockSpec((tm, d), lambda i: (i, 0)), pl.BlockSpec((1, d), lambda i: (0, 0))],
        out_specs=pl.BlockSpec((tm, d), lambda i: (i, 0)),
        out_shape=jax.ShapeDtypeStruct((m, d), out_dtype),
        compiler_params=_params("parallel"),
        name="rmsnorm",
    )(x2d, g.reshape(1, d))


def _mm_body(h_ref, w_ref, o_ref, *, act):
    acc = jnp.dot(h_ref[...], w_ref[...], preferred_element_type=F32)
    if act == "sigmoid":
        acc = _sigmoid(acc)
    o_ref[...] = acc


def _mm(h, w, act=None, tn=512):
    m, k = h.shape
    n = w.shape[1]
    tm = min(m, 1024)
    tn = min(tn, n)
    return pl.pallas_call(
        functools.partial(_mm_body, act=act),
        grid=(m // tm, n // tn),
        in_specs=[pl.BlockSpec((tm, k), lambda i, j: (i, 0)), pl.BlockSpec((k, tn), lambda i, j: (0, j))],
        out_specs=pl.BlockSpec((tm, tn), lambda i, j: (i, j)),
        out_shape=jax.ShapeDtypeStruct((m, n), F32),
        compiler_params=_params("parallel", "arbitrary"),
        name="proj_" + (act or "lin"),
    )(h, w)


def _mm_small_body(h_ref, w_ref, b_ref, o_ref):
    v = jnp.dot(h_ref[...], w_ref[...], preferred_element_type=F32) + b_ref[...]
    col = _iota(v.shape, 1)
    logsig = jnp.minimum(v, 0.0) - jnp.log1p(jnp.exp(-jnp.abs(v)))
    o_ref[...] = jnp.where(col < FOX_HEADS, logsig, _sigmoid(v))


def _mm_small(h, w, b):
    m, k = h.shape
    tm = min(m, 1024)
    return pl.pallas_call(
        _mm_small_body,
        grid=(m // tm,),
        in_specs=[pl.BlockSpec((tm, k), lambda i: (i, 0)), pl.BlockSpec((k, SMALL_W), lambda i: (0, 0)),
                  pl.BlockSpec((1, SMALL_W), lambda i: (0, 0))],
        out_specs=pl.BlockSpec((tm, SMALL_W), lambda i: (i, 0)),
        out_shape=jax.ShapeDtypeStruct((m, SMALL_W), F32),
        compiler_params=_params("parallel"),
        name="proj_small",
    )(h, w, b)


def _mm_slab_body(h_ref, w_ref, o_ref, *, nslab):
    acc = jnp.dot(h_ref[...], w_ref[...], preferred_element_type=F32)
    for c in range(nslab):
        o_ref[c] = acc[:, c * HEAD_DIM:(c + 1) * HEAD_DIM]


def _mm_slab(h, w):
    m, k = h.shape
    n = w.shape[1]
    nslab = n // HEAD_DIM
    tm = min(m, 1024)
    return pl.pallas_call(
        functools.partial(_mm_slab_body, nslab=nslab),
        grid=(m // tm,),
        in_specs=[pl.BlockSpec((tm, k), lambda i: (i, 0)), pl.BlockSpec((k, n), lambda i: (0, 0))],
        out_specs=pl.BlockSpec((nslab, tm, HEAD_DIM), lambda i: (0, i, 0)),
        out_shape=jax.ShapeDtypeStruct((nslab, m, HEAD_DIM), F32),
        compiler_params=_params("parallel"),
        name="proj_slab",
    )(h, w)


def _compress_body(x_ref, pos_ref, w1_ref, w2_ref, o_ref):
    x = x_ref[0] + pos_ref[0]
    h1 = jnp.dot(x.astype(BF16), w1_ref[0], preferred_element_type=F32)
    a = h1 * _sigmoid(h1)
    o_ref[0] = jnp.dot(a.astype(BF16), w2_ref[0], preferred_element_type=F32)


def _compress(xc, pos, w1, w2):
    nslab, r, kk = xc.shape
    tr = min(r, 256)
    kind = lambda c, i: (c // NSA_KV_HEADS, 0, 0)
    return pl.pallas_call(
        _compress_body,
        grid=(nslab, r // tr),
        in_specs=[pl.BlockSpec((1, tr, kk), lambda c, i: (c, i, 0)),
                  pl.BlockSpec((1, 1, kk), kind),
                  pl.BlockSpec((1, kk, HEAD_DIM), kind),
                  pl.BlockSpec((1, HEAD_DIM, HEAD_DIM), kind)],
        out_specs=pl.BlockSpec((1, tr, HEAD_DIM), lambda c, i: (c, i, 0)),
        out_shape=jax.ShapeDtypeStruct((nslab, r, HEAD_DIM), F32),
        compiler_params=_params("parallel", "parallel"),
        name="nsa_compress",
    )(xc, pos, w1, w2)


def _cumsum_body(lf_ref, ccol_ref, crow_ref, *, t):
    tri = (_iota((128, 128), 0) >= _iota((128, 128), 1)).astype(F32)
    carry = jnp.zeros((1, SMALL_W), F32)
    for ch in range(t // 128):
        rows = slice(ch * 128, (ch + 1) * 128)
        cs = jnp.dot(tri, lf_ref[rows, :], precision=HIGHEST, preferred_element_type=F32) + carry
        ccol_ref[rows, :] = cs
        crow_ref[0, :, rows] = cs.T[0:FOX_HEADS, :]
        carry = cs[127:128, :]


def _cumsum_logf(small, b, t):
    return pl.pallas_call(
        functools.partial(_cumsum_body, t=t),
        grid=(b,),
        in_specs=[pl.BlockSpec((t, SMALL_W), lambda i: (i, 0))],
        out_specs=[pl.BlockSpec((t, SMALL_W), lambda i: (i, 0)),
                   pl.BlockSpec((1, FOX_HEADS, t), lambda i: (i, 0, 0))],
        out_shape=[jax.ShapeDtypeStruct((b * t, SMALL_W), F32), jax.ShapeDtypeStruct((b, FOX_HEADS, t), F32)],
        compiler_params=_params("parallel"),
        name="fox_cumsum",
    )(small)


def _flash_loop(q, k_ref, v_ref, spans, tk, score_fn, select, m_sc, l_sc, acc_sc):
    m_sc[...] = jnp.full(m_sc.shape, NEG, F32)
    l_sc[...] = jnp.zeros(l_sc.shape, F32)
    acc_sc[...] = jnp.zeros(acc_sc.shape, F32)

    def make_body(masked):
        def body(kt, carry):
            k0 = pl.multiple_of(kt * tk, tk)
            kb = k_ref[pl.ds(k0, tk), :].astype(BF16)
            vb = v_ref[pl.ds(k0, tk), :].astype(BF16)
            s = lax.dot_general(q, kb, NT_DIMS, preferred_element_type=F32)
            s, row_c, mask = score_fn(s, kt, k0, masked)
            if mask is not None:
                s = select(mask, s, NEG)
            m_old = m_sc[...]
            m_new = jnp.maximum(m_old, jnp.max(s, axis=-1, keepdims=True) + row_c)
            alpha = jnp.exp2(m_old - m_new)
            p = jnp.exp2(s - (m_new - row_c))
            if mask is not None:
                p = select(mask, p, 0.0)
            l_sc[...] = alpha * l_sc[...] + jnp.sum(p, axis=-1, keepdims=True)
            acc_sc[...] = alpha * acc_sc[...] + jnp.dot(p.astype(BF16), vb, preferred_element_type=F32)
            m_sc[...] = m_new
            return carry
        return body

    for lo, hi, masked in spans:
        lax.fori_loop(lo, hi, make_body(masked), 0)
    return acc_sc[...] / jnp.maximum(l_sc[...], 1e-30)


def _fox_body(q_ref, k_ref, v_ref, ccol_ref, crow_ref, o_ref, m_sc, l_sc, acc_sc, *, tq, tk):
    h = pl.program_id(1)
    q0 = pl.program_id(2) * tq
    q = (q_ref[...] * (SCALE * LOG2E)).astype(BF16)
    lane = _iota((tq, SMALL_W), 1)
    cq = jnp.sum(jnp.where(lane == h, ccol_ref[...], 0.0), axis=-1, keepdims=True) * LOG2E
    qpos = q0 + _iota((tq, 1), 0)
    kloc = _iota((1, tk), 1)

    def score_fn(s, kt, k0, masked):
        ck = crow_ref[0, 0, kt] * LOG2E
        return s - ck, cq, ((k0 + kloc) <= qpos) if masked else None

    assert tq == tk
    kt_diag = q0 // tk
    o_ref[...] = _flash_loop(q, k_ref, v_ref, [(0, kt_diag, False), (kt_diag, kt_diag + 1, True)], tk,
                             score_fn, jnp.where, m_sc, l_sc, acc_sc)


def _fox_prompt(fq, fkv, ccol, crow, b, t):
    tq = tk = 512
    nq, nkt = t // tq, t // tk
    crow5 = crow.reshape(b, FOX_HEADS, nkt, 1, tk)
    return pl.pallas_call(
        functools.partial(_fox_body, tq=tq, tk=tk),
        grid=(b, FOX_HEADS, nq),
        in_specs=[pl.BlockSpec((tq, HEAD_DIM), lambda i, h, j: (i * nq + j, h)),
                  pl.BlockSpec((t, HEAD_DIM), lambda i, h, j: (i, h)),
                  pl.BlockSpec((t, HEAD_DIM), lambda i, h, j: (i, FOX_HEADS + h)),
                  pl.BlockSpec((tq, SMALL_W), lambda i, h, j: (i * nq + j, 0)),
                  pl.BlockSpec((1, 1, nkt, 1, tk), lambda i, h, j: (i, h, 0, 0, 0))],
        out_specs=pl.BlockSpec((tq, HEAD_DIM), lambda i, h, j: (i * nq + j, h)),
        out_shape=jax.ShapeDtypeStruct((b * t, FOX_WIDTH), F32),
        scratch_shapes=[pltpu.VMEM((tq, 1), F32), pltpu.VMEM((tq, 1), F32), pltpu.VMEM((tq, HEAD_DIM), F32)],
        compiler_params=_params("parallel", "parallel", "arbitrary"),
        name="fox_prompt",
    )(fq, fkv, fkv, ccol, crow5)


def _masked_softmax(s, mask, axis):
    s = jnp.where(mask, s, NEG)
    m = jnp.max(s, axis=axis, keepdims=True)
    e = jnp.where(mask, jnp.exp(s - m), 0.0)
    return e / jnp.maximum(jnp.sum(e, axis=axis, keepdims=True), 1e-30)


def _group_slope_scale(g):
    scale = jnp.float32(1.0)
    for gg in range(1, NSA_KV_HEADS):
        scale = jnp.where(g == gg, jnp.float32(ALIBI_START ** (NSA_HPG * gg)), scale)
    return scale


def _head_slopes(head_idx, n):
    out = jnp.zeros(head_idx.shape, F32)
    for hh in range(n):
        out = jnp.where(head_idx == hh, jnp.float32(ALIBI_START ** (hh + 1)), out)
    return out


def _nsa_body(q_ref, kc_ref, vc_ref, ks_ref, vs_ref, kw_ref, vw_ref, ocmp_ref, oslc_ref, owin_ref,
              m_sc, l_sc, acc_sc, sel_sc, *, tq, tk, n_blk):
    g = pl.program_id(1)
    q0 = pl.program_id(2) * tq
    rows = NSA_HPG * tq
    lg_tq = tq.bit_length() - 1
    qf = q_ref[...]
    q4f = jnp.concatenate([qf[:, j * HEAD_DIM:(j + 1) * HEAD_DIM] for j in range(NSA_HPG)], axis=0)
    q4 = q4f.astype(BF16)
    q4s = (q4f * (SCALE * LOG2E)).astype(BF16)
    gscale = _group_slope_scale(g)

    def unstack(o):
        return jnp.concatenate([o[j * tq:(j + 1) * tq] for j in range(NSA_HPG)], axis=1)

    ri = _iota((rows, 1), 0)
    qpos = q0 + (ri & (tq - 1))
    slope = _head_slopes(ri >> lg_tq, NSA_HPG) * gscale
    kc = kc_ref[0].astype(BF16)
    vc = vc_ref[0].astype(BF16)
    cend = (_iota((1, n_blk), 1) + 1) * CMP_BLOCK - 1
    dc = qpos - cend
    sc = lax.dot_general(q4, kc, NT_DIMS, preferred_element_type=F32) * SCALE - slope * dc.astype(F32)
    p = _masked_softmax(sc, dc >= 0, -1)
    ocmp_ref[...] = unstack(jnp.dot(p.astype(BF16), vc, preferred_element_type=F32))

    ci = _iota((1, rows), 1)
    qpos_t = q0 + (ci & (tq - 1))
    slope_t = _head_slopes(ci >> lg_tq, NSA_HPG) * gscale
    blk = _iota((n_blk, 1), 0)
    dct = qpos_t - ((blk + 1) * CMP_BLOCK - 1)
    sct = lax.dot_general(kc, q4, NT_DIMS, preferred_element_type=F32) * SCALE - slope_t * dct.astype(F32)
    pt = _masked_softmax(sct, dct >= 0, 0)
    imp = pt[:, 0:tq]
    for j in range(1, NSA_HPG):
        imp = imp + pt[:, j * tq:(j + 1) * tq]
    own = _div_pow2(q0 + _iota((1, tq), 1), SEL_BLOCK)
    forced = ((blk <= own) & (blk > own - N_FORCED_LOCAL)) | (blk == 0)
    imp = jnp.where(forced, 1e9, jnp.where(blk > own, -1e9, imp))
    rank = jnp.zeros((n_blk, tq), F32)
    for mm in range(n_blk):
        row = imp[mm:mm + 1, :]
        rank = rank + jnp.where(blk > mm, (row >= imp).astype(F32), (row > imp).astype(F32))
    sel_sc[...] = (rank < TOP_N).astype(F32)

    kloc = _iota((1, tk), 1)
    qpos_q = q0 + _iota((tq, 1), 0)
    slope2 = slope * LOG2E
    bias0 = slope2 * (qpos - kloc).astype(F32)
    e_row = _iota((n_blk, tk), 0)
    e_blk = _div_pow2(_iota((n_blk, tk), 1), SEL_BLOCK)

    def alibi(s, k0):
        return s - bias0, slope2 * k0.astype(F32)

    def select(mask_q, x, fill):
        return jnp.where(mask_q[None], x.reshape(NSA_HPG, tq, tk), fill).reshape(rows, tk)

    def slc_score(s, kt, k0, masked):
        expand = (e_row == kt * (tk // SEL_BLOCK) + e_blk).astype(BF16)
        picked = lax.dot_general(sel_sc[...].astype(BF16), expand, TN_DIMS, preferred_element_type=F32)
        mask_q = (picked > 0.5) & ((k0 + kloc) <= qpos_q)
        return alibi(s, k0) + (mask_q,)

    def win_score(s, kt, k0, masked):
        dist = qpos_q - (k0 + kloc)
        mask_q = (dist >= 0) & (dist < WINDOW)
        return alibi(s, k0) + (mask_q,)

    kt_hi = q0 // tk + 1
    oslc_ref[...] = unstack(_flash_loop(q4s, ks_ref, vs_ref, [(0, kt_hi, True)], tk, slc_score, select,
                                        m_sc, l_sc, acc_sc))
    kt_lo = jnp.maximum(q0 - (WINDOW - 1), 0) // tk
    owin_ref[...] = unstack(_flash_loop(q4s, kw_ref, vw_ref, [(kt_lo, kt_hi, True)], tk, win_score, select,
                                        m_sc, l_sc, acc_sc))


def _nsa_prompt(nq, cmpkv, nkv, nwkv, b, t):
    tq, tk = 128, 512
    nqt = t // tq
    n_blk = t // CMP_BLOCK
    gw = NSA_HPG * HEAD_DIM
    rows = NSA_HPG * tq
    qmap = lambda i, g, j: (i * nqt + j, g)
    col = lambda c: (lambda i, g, j: (i, c * NSA_KV_HEADS + g))
    slab = lambda c: (lambda i, g, j: (c * NSA_KV_HEADS + g, i, 0))
    out = jax.ShapeDtypeStruct((b * t, NSA_WIDTH), F32)
    return pl.pallas_call(
        functools.partial(_nsa_body, tq=tq, tk=tk, n_blk=n_blk),
        grid=(b, NSA_KV_HEADS, nqt),
        in_specs=[pl.BlockSpec((tq, gw), qmap),
                  pl.BlockSpec((1, n_blk, HEAD_DIM), slab(0)),
                  pl.BlockSpec((1, n_blk, HEAD_DIM), slab(1)),
                  pl.BlockSpec((t, HEAD_DIM), col(2)),
                  pl.BlockSpec((t, HEAD_DIM), col(3)),
                  pl.BlockSpec((t, HEAD_DIM), col(0)),
                  pl.BlockSpec((t, HEAD_DIM), col(1))],
        out_specs=[pl.BlockSpec((tq, gw), qmap)] * 3,
        out_shape=[out, out, out],
        scratch_shapes=[pltpu.VMEM((rows, 1), F32), pltpu.VMEM((rows, 1), F32), pltpu.VMEM((rows, HEAD_DIM), F32),
                        pltpu.VMEM((n_blk, tq), F32)],
        compiler_params=_params("parallel", "parallel", "arbitrary"),
        name="nsa_prompt",
    )(nq, cmpkv, cmpkv, nkv, nkv, nwkv, nwkv)


def _merge_body(x_ref, of_ref, fg_ref, oc_ref, os_ref, ow_ref, ng_ref, sm_ref, mg_ref, wbf_ref, wbn_ref, wo_ref,
                gf_ref, o_ref, *, final):
    fg = fg_ref[...]
    ya = jnp.dot((of_ref[...] * (fg * _sigmoid(fg))).astype(BF16), wbf_ref[...], preferred_element_type=F32)
    sm = sm_ref[...]
    parts = []
    for hd in range(NSA_HEADS):
        cols = slice(hd * HEAD_DIM, (hd + 1) * HEAD_DIM)
        gate = lambda br: sm[:, GATE_COL + br * NSA_HEADS + hd:GATE_COL + br * NSA_HEADS + hd + 1]
        parts.append(gate(0) * oc_ref[:, cols] + gate(1) * os_ref[:, cols] + gate(2) * ow_ref[:, cols])
    ng = ng_ref[...]
    on = jnp.concatenate(parts, axis=1) * (ng * _sigmoid(ng))
    yb = jnp.dot(on.astype(BF16), wbn_ref[...], preferred_element_type=F32)
    mix = mg_ref[:, 0:D_MODEL] * ya + mg_ref[:, D_MODEL:2 * D_MODEL] * yb
    xn = x_ref[...] + jnp.dot(mix.astype(BF16), wo_ref[...], preferred_element_type=F32)
    if final:
        ms = jnp.mean(xn * xn, axis=-1, keepdims=True)
        xn = xn * lax.rsqrt(ms + RMS_EPS) * gf_ref[...]
    o_ref[...] = xn


def _merge(x, o_fox, fgate, o_cmp, o_slc, o_win, ngate, small, mg, w_bf, w_bn, w_o, g_final, final):
    m = x.shape[0]
    tm = min(m, 128)
    row = lambda w: pl.BlockSpec((tm, w), lambda i: (i, 0))
    whole = lambda a: pl.BlockSpec(a.shape, lambda i: (0, 0), pipeline_mode=pl.Buffered(1))
    gf = g_final.reshape(1, D_MODEL)
    return pl.pallas_call(
        functools.partial(_merge_body, final=final),
        grid=(m // tm,),
        in_specs=[row(D_MODEL), row(FOX_WIDTH), row(FOX_WIDTH), row(NSA_WIDTH), row(NSA_WIDTH), row(NSA_WIDTH),
                  row(NSA_WIDTH), row(SMALL_W), row(2 * D_MODEL), whole(w_bf), whole(w_bn), whole(w_o), whole(gf)],
        out_specs=row(D_MODEL),
        out_shape=jax.ShapeDtypeStruct((m, D_MODEL), F32),
        compiler_params=_params("parallel"),
        name="merge_final" if final else "merge",
    )(x, o_fox, fgate, o_cmp, o_slc, o_win, ngate, small, mg, w_bf, w_bn, w_o, gf)


ROWS_PER_TOKEN = 8
assert ROWS_PER_TOKEN == FOX_HEADS == NSA_HEADS == 4 * NSA_KV_HEADS
CMP_CHUNK = 8
assert CMP_BLOCK % CMP_CHUNK == 0


def _rep_rows(col, n):
    return jnp.concatenate([col] * n, axis=0)


def _paged_softmax(val, mask, s_new, n_pages):
    val = jnp.where(mask, val, NEG)
    m_rows = jnp.max(val, axis=-1, keepdims=True)
    m = jnp.maximum(jnp.max(m_rows.reshape(n_pages, ROWS_PER_TOKEN, 1), axis=0), s_new)
    e = jnp.where(mask, jnp.exp(val - _rep_rows(m, n_pages)), 0.0)
    e_new = jnp.exp(s_new - m)
    l_rows = jnp.sum(e, axis=-1, keepdims=True)
    inv = 1.0 / jnp.maximum(jnp.sum(l_rows.reshape(n_pages, ROWS_PER_TOKEN, 1), axis=0) + e_new, 1e-30)
    return e * _rep_rows(inv, n_pages), e_new * inv


def _fox_dec_body(pt_ref, *refs, n_pages, page):
    kv = refs[:n_pages]
    lt = refs[n_pages:2 * n_pages]
    q_ref, kvn_ref, lfn_ref, o_ref = refs[2 * n_pages:]
    hh = ROWS_PER_TOKEN
    rows = n_pages * hh
    lanes = page * hh
    q = q_ref[...]
    qb = q.astype(BF16)
    k_new = kvn_ref[0]
    v_new = kvn_ref[1]
    s_new = jnp.sum(q * k_new, axis=-1, keepdims=True) * SCALE

    lt_all = jnp.concatenate([lt[j][...] for j in range(n_pages)], axis=0)
    later = (_iota((page, page), 0) > _iota((page, page), 1)).astype(F32)
    within = jnp.dot(lt_all, later, precision=HIGHEST, preferred_element_type=F32)
    tot = jnp.broadcast_to(jnp.sum(lt_all, axis=-1, keepdims=True), (rows, page))
    rr = _iota((rows, rows), 0)
    cc = _iota((rows, rows), 1)
    after = (((cc & (hh - 1)) == (rr & (hh - 1))) & (cc > rr)).astype(F32)
    decay = (within + jnp.dot(after, tot, precision=HIGHEST, preferred_element_type=F32)
             + _rep_rows(lfn_ref[...], n_pages))
    hi = decay.astype(BF16)
    r1 = decay - hi.astype(F32)
    mid = r1.astype(BF16)
    lo = (r1 - mid.astype(F32)).astype(BF16)
    spread = (_iota((page, lanes), 0) == _div_pow2(_iota((page, lanes), 1), hh)).astype(BF16)
    d3 = jnp.dot(jnp.concatenate([hi, mid, lo], axis=0), spread, preferred_element_type=F32)
    decay_l = d3[0:rows] + d3[rows:2 * rows] + d3[2 * rows:3 * rows]

    s_all = jnp.concatenate(
        [lax.dot_general(qb, kv[j][:, 0].reshape(lanes, HEAD_DIM).astype(BF16), NT_DIMS, preferred_element_type=F32)
         for j in range(n_pages)], axis=0)
    own_head = (_iota((rows, 1), 0) & (hh - 1)) == (_iota((1, lanes), 1) & (hh - 1))
    p, p_new = _paged_softmax(s_all * SCALE + decay_l, own_head, s_new, n_pages)
    pb = p.astype(BF16)
    acc = p_new * v_new
    for j in range(n_pages):
        vb = kv[j][:, 1].reshape(lanes, HEAD_DIM).astype(BF16)
        acc = acc + jnp.dot(pb[j * hh:(j + 1) * hh], vb, preferred_element_type=F32)
    o_ref[...] = acc


def _fox_decode(cache_kv, cache_lf_t, page_table, layer, fq, fkv, small):
    s, n_pages = page_table.shape
    page = cache_kv.shape[2]
    hh = ROWS_PER_TOKEN
    kv_specs = [pl.BlockSpec((None, None, page, 2, hh, HEAD_DIM), lambda i, pt, j=j: (layer, pt[i, j], 0, 0, 0, 0))
                for j in range(n_pages)]
    lt_specs = [pl.BlockSpec((None, None, hh, page), lambda i, pt, j=j: (layer, pt[i, j], 0, 0))
                for j in range(n_pages)]
    out = pl.pallas_call(
        functools.partial(_fox_dec_body, n_pages=n_pages, page=page),
        grid_spec=pltpu.PrefetchScalarGridSpec(
            num_scalar_prefetch=1,
            grid=(s,),
            in_specs=kv_specs + lt_specs
            + [pl.BlockSpec((None, hh, HEAD_DIM), lambda i, pt: (i, 0, 0)),
               pl.BlockSpec((None, 2, hh, HEAD_DIM), lambda i, pt: (i, 0, 0, 0)),
               pl.BlockSpec((None, hh, 1), lambda i, pt: (i, 0, 0))],
            out_specs=pl.BlockSpec((None, hh, HEAD_DIM), lambda i, pt: (i, 0, 0)),
        ),
        out_shape=jax.ShapeDtypeStruct((s, hh, HEAD_DIM), F32),
        compiler_params=_params("arbitrary"),
        name="fox_decode",
    )(page_table, *([cache_kv] * n_pages), *([cache_lf_t] * n_pages),
      fq.reshape(s, hh, HEAD_DIM), fkv.reshape(s, 2, hh, HEAD_DIM), small[:, 0:hh].reshape(s, hh, 1))
    return out.reshape(s, FOX_WIDTH)


def _nsa_dec_body(pt_ref, *refs, n_pages, page, win_buf):
    pg = refs[:n_pages]
    (win_ref, q_ref, kvn_ref, wn_ref, pos_ref, w1_ref, w2_ref, ocmp_ref, oslc_ref, owin_ref, acc_sc) = refs[n_pages:]
    nk = ROWS_PER_TOKEN
    gg_n = NSA_KV_HEADS
    past = n_pages * page
    n_blk = past // CMP_BLOCK
    own = past // SEL_BLOCK
    bpp = page // CMP_BLOCK
    q = q_ref[...]
    qb = q.astype(BF16)
    head = _iota((NSA_HEADS, 1), 0)
    grp = _div_pow2(head, NSA_HPG)
    slope = _head_slopes(head, NSA_HEADS)

    def group_row(a, first):
        out = jnp.zeros((NSA_HEADS, HEAD_DIM), F32)
        for gg in range(gg_n):
            out = jnp.where(grp == gg, jnp.broadcast_to(a[first + gg:first + gg + 1, :], (NSA_HEADS, HEAD_DIM)), out)
        return out

    acc_sc[...] = jnp.zeros(acc_sc.shape, F32)

    def cbody(ic, carry):
        xs = []
        for ii in range(CMP_CHUNK):
            i = ic * CMP_CHUNK + ii
            pos_i = pos_ref[i]
            tiles = []
            for j in range(n_pages):
                for hb in range(bpp):
                    r0 = pl.multiple_of((hb * CMP_BLOCK + i) * nk, nk)
                    tiles.append(pg[j][pl.ds(r0, nk), :] + pos_i)
            xs.append(jnp.concatenate(tiles, axis=0).astype(BF16))
        x = jnp.concatenate(xs, axis=1)
        acc_sc[...] += jnp.dot(x, w1_ref[ic], preferred_element_type=F32)
        return carry

    lax.fori_loop(0, CMP_BLOCK // CMP_CHUNK, cbody, 0)
    krow = _iota((n_blk * nk, 1), 0) & (nk - 1)
    h1 = acc_sc[...]
    h1 = jnp.where(krow < gg_n, h1[:, 0:HEAD_DIM], h1[:, HEAD_DIM:2 * HEAD_DIM])
    c2 = jnp.dot((h1 * _sigmoid(h1)).astype(BF16), w2_ref[...], preferred_element_type=F32)
    cm = jnp.where(krow < gg_n, c2[:, 0:HEAD_DIM], c2[:, HEAD_DIM:2 * HEAD_DIM])
    cmb = jnp.where(krow < 2 * gg_n, cm, 0.0).astype(BF16)

    wc = n_blk * nk
    ln = _iota((1, wc), 1)
    nidx = _div_pow2(ln, nk)
    kk = ln & (nk - 1)
    dc = past - ((nidx + 1) * CMP_BLOCK - 1)
    sc = lax.dot_general(qb, cmb, NT_DIMS, preferred_element_type=F32) * SCALE - slope * dc.astype(F32)
    p = _masked_softmax(sc, (kk == grp) & (dc >= 0), -1)
    ocmp_ref[...] = jnp.dot(pltpu.roll(p, gg_n, axis=1).astype(BF16), cmb, preferred_element_type=F32)

    imp = jnp.sum(p, axis=0, keepdims=True)
    forced = (nidx == 0) | ((nidx <= own) & (nidx > own - N_FORCED_LOCAL))
    imp = jnp.where(forced, 1e9, imp)
    rr = _iota((wc, wc), 0)
    cc = _iota((wc, wc), 1)
    imp_b = jnp.broadcast_to(imp, (wc, wc))
    imp_col = jnp.sum(jnp.where(rr == cc, imp_b, 0.0), axis=-1, keepdims=True)
    beats = jnp.where(rr < cc, (imp_col >= imp_b).astype(F32), (imp_col > imp_b).astype(F32))
    beats = jnp.where((rr & (nk - 1)) == (cc & (nk - 1)), beats, 0.0)
    rank = jnp.sum(beats, axis=0, keepdims=True) + jnp.where(forced, 0.0, 1.0)
    sel = jnp.where((rank < TOP_N) & (kk < gg_n), 1.0, 0.0)
    pick = (_div_pow2(_iota((wc, n_blk), 0), nk) == _iota((wc, n_blk), 1)).astype(BF16)
    sel_hb = jnp.dot(jnp.where(kk == grp, jnp.broadcast_to(sel, (NSA_HEADS, wc)), 0.0).astype(BF16), pick,
                     preferred_element_type=F32)

    lanes = page * nk
    rows = n_pages * NSA_HEADS
    pb = [pg[j][...].astype(BF16) for j in range(n_pages)]
    s_all = jnp.concatenate([lax.dot_general(qb, pb[j], NT_DIMS, preferred_element_type=F32)
                             for j in range(n_pages)], axis=0)
    ri = _iota((rows, 1), 0)
    crow = ri & (NSA_HEADS - 1)
    ll = _iota((1, lanes), 1)
    tt = _div_pow2(ll, nk)
    dist = past - (_div_pow2(ri, NSA_HEADS) * page + tt)
    val = s_all * SCALE - _head_slopes(crow, NSA_HEADS) * dist.astype(F32)
    tblk = _div_pow2(tt, SEL_BLOCK)
    picked = []
    for j in range(n_pages):
        m_j = jnp.zeros((NSA_HEADS, lanes), F32)
        for hb in range(bpp):
            n = j * bpp + hb
            m_j = jnp.where(tblk == hb, sel_hb[:, n:n + 1], m_j)
        picked.append(m_j)
    key_rows = (ll & (nk - 1)) == 2 * gg_n + _div_pow2(crow, NSA_HPG)
    mask = key_rows & (jnp.concatenate(picked, axis=0) > 0.5)
    kvn = kvn_ref[...]
    s_new = jnp.sum(q * group_row(kvn, 2 * gg_n), axis=-1, keepdims=True) * SCALE
    ps, ps_new = _paged_softmax(val, mask, s_new, n_pages)
    psb = pltpu.roll(ps, gg_n, axis=1).astype(BF16)
    acc = ps_new * group_row(kvn, 3 * gg_n)
    for j in range(n_pages):
        acc = acc + jnp.dot(psb[j * NSA_HEADS:(j + 1) * NSA_HEADS], pb[j], preferred_element_type=F32)
    oslc_ref[...] = acc

    wr = 2 * gg_n
    wb = win_ref[...].astype(BF16)
    lw = _iota((1, win_buf * wr), 1)
    wdist = win_buf - _div_pow2(lw, wr)
    sw = lax.dot_general(qb, wb, NT_DIMS, preferred_element_type=F32) * SCALE - slope * wdist.astype(F32)
    wmask = ((lw & (wr - 1)) == grp) & (wdist < WINDOW) & (past - wdist >= 0)
    wn = wn_ref[...]
    sw_new = jnp.sum(q * group_row(wn, 0), axis=-1, keepdims=True) * SCALE
    pw, pw_new = _paged_softmax(sw, wmask, sw_new, 1)
    owin_ref[...] = pw_new * group_row(wn, gg_n) + jnp.dot(pltpu.roll(pw, gg_n, axis=1).astype(BF16), wb,
                                                           preferred_element_type=F32)


def _nsa_decode(cache_rows, win_rows, page_table, layer, nq, nkv, nwkv, pos8, w1cat, w2cat):
    s, n_pages = page_table.shape
    nk = ROWS_PER_TOKEN
    page = cache_rows.shape[2] // nk
    wr = 2 * NSA_KV_HEADS
    win_buf = win_rows.shape[2] // wr
    n_blk = n_pages * page // CMP_BLOCK
    pg_specs = [pl.BlockSpec((None, None, page * nk, HEAD_DIM), lambda i, pt, j=j: (layer, pt[i, j], 0, 0))
                for j in range(n_pages)]
    whole = lambda a: pl.BlockSpec(a.shape, lambda i, pt: (0,) * a.ndim, pipeline_mode=pl.Buffered(1))
    head_rows = lambda n: pl.BlockSpec((None, n, HEAD_DIM), lambda i, pt: (i, 0, 0))
    out = jax.ShapeDtypeStruct((s, NSA_HEADS, HEAD_DIM), F32)
    outs = pl.pallas_call(
        functools.partial(_nsa_dec_body, n_pages=n_pages, page=page, win_buf=win_buf),
        grid_spec=pltpu.PrefetchScalarGridSpec(
            num_scalar_prefetch=1,
            grid=(s,),
            in_specs=pg_specs
            + [pl.BlockSpec((None, None, win_buf * wr, HEAD_DIM), lambda i, pt: (layer, i, 0, 0)),
               head_rows(NSA_HEADS), head_rows(nk), head_rows(wr), whole(pos8), whole(w1cat), whole(w2cat)],
            out_specs=[head_rows(NSA_HEADS)] * 3,
            scratch_shapes=[pltpu.VMEM((n_blk * nk, 2 * HEAD_DIM), F32)],
        ),
        out_shape=[out, out, out],
        compiler_params=_params("arbitrary"),
        name="nsa_decode",
    )(page_table, *([cache_rows] * n_pages), win_rows,
      nq.reshape(s, NSA_HEADS, HEAD_DIM), nkv.reshape(s, nk, HEAD_DIM), nwkv.reshape(s, wr, HEAD_DIM),
      pos8, w1cat, w2cat)
    return [o.reshape(s, NSA_WIDTH) for o in outs]


def _decode_compress_weights(w_cmp1_l, w_cmp2_l, cmp_pos_l):
    gg_n = NSA_KV_HEADS
    w1cat = w_cmp1_l.reshape(2, CMP_BLOCK, HEAD_DIM, HEAD_DIM).transpose(1, 2, 0, 3)
    w1cat = w1cat.reshape(CMP_BLOCK // CMP_CHUNK, CMP_CHUNK * HEAD_DIM, 2 * HEAD_DIM).astype(BF16)
    w2cat = jnp.concatenate([w_cmp2_l[0], w_cmp2_l[1]], axis=1).astype(BF16)
    pos = jnp.repeat(cmp_pos_l.transpose(1, 0, 2), gg_n, axis=1)
    pos8 = jnp.concatenate([pos, jnp.zeros((CMP_BLOCK, ROWS_PER_TOKEN - 2 * gg_n, HEAD_DIM), F32)], axis=1)
    return pos8, w1cat, w2cat


def _split_weights(w_in_l, b_f_l):
    off = [int(v) for v in np.cumsum((0,) + SPLIT_SIZES)]
    cols = lambda i: w_in_l[:, off[i]:off[i + 1]]
    names = ("fq", "fkv", "ff", "fgate", "nq", "nkv", "nwkv", "nbg", "ngate", "mg")
    w = {n: cols(i) for i, n in enumerate(names)}
    pad = SMALL_W - FOX_HEADS - 3 * NSA_HEADS
    small = jnp.concatenate([w.pop("ff"), w.pop("nbg"), jnp.zeros((D_MODEL, pad), F32)], axis=1)
    out = {n: v.astype(BF16) for n, v in w.items()}
    out["small"] = small.astype(BF16)
    out["b_small"] = jnp.concatenate([b_f_l, jnp.zeros((SMALL_W - FOX_HEADS,), F32)]).reshape(1, SMALL_W)
    out["cmp"] = out["nkv"][:, 0:2 * NSA_KV_HEADS * HEAD_DIM]
    return out


def _project(x2d, g, w):
    h = _rmsnorm(x2d, g, BF16, min(x2d.shape[0], 512))
    pr = {n: _mm(h, w[n]) for n in ("fq", "fkv", "fgate", "nq", "nkv", "nwkv", "ngate")}
    pr["mg"] = _mm(h, w["mg"], act="sigmoid")
    pr["small"] = _mm_small(h, w["small"], w["b_small"])
    return h, pr


def kernel(x_prompt, x_sample, cache_fox_kv, cache_fox_logf, cache_nsa_kv, cache_nsa_win, page_table,
           norm_g, w_in, b_fox_f, w_cmp1, w_cmp2, cmp_pos, w_branch_fox, w_branch_nsa, w_out, final_norm_g):
    b, t, d = x_prompt.shape
    s, dec_seq, _ = x_sample.shape
    depth, n_phys, page = cache_fox_kv.shape[:3]
    n_pages = page_table.shape[1]
    win_buf = cache_nsa_win.shape[2]
    win_p = min(WINDOW, t)
    assert dec_seq == 1 and d == D_MODEL and t % 512 == 0 and (n_pages * page) % SEL_BLOCK == 0
    assert NSA_KV_HEADS == 2 and win_buf <= n_pages * page and page % CMP_BLOCK == 0

    cache_rows = cache_nsa_kv.reshape(depth, n_phys, page * ROWS_PER_TOKEN, HEAD_DIM)
    win_rows = cache_nsa_win.reshape(depth, s, win_buf * 2 * NSA_KV_HEADS, HEAD_DIM)
    cache_lf_t = jnp.swapaxes(cache_fox_logf, 2, 3)
    page_table = page_table.astype(jnp.int32)

    xp = x_prompt.reshape(b * t, d)
    xs = x_sample.reshape(s, d)
    outs = {k: [] for k in ("fkv_p", "logf_p", "nkv_p", "win_p", "fkv_s", "logf_s", "nkv_s", "win_new")}
    for l in range(depth):
        final = l == depth - 1
        w = _split_weights(w_in[l], b_fox_f[l])
        w_bf = w_branch_fox[l].astype(BF16)
        w_bn = w_branch_nsa[l].astype(BF16)
        w_o = w_out[l].astype(BF16)
        w1 = w_cmp1[l].astype(BF16)
        w2 = w_cmp2[l].astype(BF16)
        pos = cmp_pos[l].reshape(2, 1, CMP_BLOCK * HEAD_DIM)

        h, pr = _project(xp, norm_g[l], w)
        ccol, crow = _cumsum_logf(pr["small"], b, t)
        o_fox = _fox_prompt(pr["fq"], pr["fkv"], ccol, crow, b, t)
        xc = _mm_slab(h, w["cmp"]).reshape(2 * NSA_KV_HEADS, b * (t // CMP_BLOCK), CMP_BLOCK * HEAD_DIM)
        cmpkv = _compress(xc, pos, w1, w2)
        o_cmp, o_slc, o_win = _nsa_prompt(pr["nq"], cmpkv, pr["nkv"], pr["nwkv"], b, t)
        xp = _merge(xp, o_fox, pr["fgate"], o_cmp, o_slc, o_win, pr["ngate"], pr["small"], pr["mg"],
                    w_bf, w_bn, w_o, final_norm_g, final)
        outs["fkv_p"].append(pr["fkv"].reshape(b, t, 2, FOX_HEADS, HEAD_DIM))
        outs["logf_p"].append(pr["small"][:, 0:FOX_HEADS].reshape(b, t, FOX_HEADS))
        outs["nkv_p"].append(pr["nkv"].reshape(b, t, 4, NSA_KV_HEADS, HEAD_DIM))
        outs["win_p"].append(pr["nwkv"].reshape(b, t, 2, NSA_KV_HEADS, HEAD_DIM)[:, t - win_p:])

        _, ps = _project(xs, norm_g[l], w)
        o_fox = _fox_decode(cache_fox_kv, cache_lf_t, page_table, l, ps["fq"], ps["fkv"], ps["small"])
        o_cmp, o_slc, o_win = _nsa_decode(cache_rows, win_rows, page_table, l, ps["nq"], ps["nkv"], ps["nwkv"],
                                          *_decode_compress_weights(w_cmp1[l], w_cmp2[l], cmp_pos[l]))
        xs = _merge(xs, o_fox, ps["fgate"], o_cmp, o_slc, o_win, ps["ngate"], ps["small"], ps["mg"],
                    w_bf, w_bn, w_o, final_norm_g, final)
        outs["fkv_s"].append(ps["fkv"].reshape(s, 1, 2, FOX_HEADS, HEAD_DIM))
        outs["logf_s"].append(ps["small"][:, 0:FOX_HEADS].reshape(s, 1, FOX_HEADS))
        outs["nkv_s"].append(ps["nkv"].reshape(s, 1, 4, NSA_KV_HEADS, HEAD_DIM))
        outs["win_new"].append(ps["nwkv"].reshape(s, 1, 2, NSA_KV_HEADS, HEAD_DIM))

    st = {k: jnp.stack(v, axis=0) for k, v in outs.items()}
    win_s = jnp.concatenate([cache_nsa_win[:, :, 1:], st["win_new"]], axis=2)
    return (xp.reshape(b, t, d), xs.reshape(s, 1, d), st["fkv_p"], st["logf_p"], st["nkv_p"], st["win_p"],
            st["fkv_s"], st["logf_s"], st["nkv_s"], win_s)
```

```python
import functools

import numpy as np
import jax
import jax.numpy as jnp
from jax import lax
from jax.experimental import pallas as pl
from jax.experimental.pallas import tpu as pltpu

F32 = jnp.float32
BF16 = jnp.bfloat16
HIGHEST = lax.Precision.HIGHEST

D_MODEL = 2048
HEAD_DIM = 128
FOX_HEADS = 8
FOX_WIDTH = FOX_HEADS * HEAD_DIM
NSA_HEADS = 8
NSA_KV_HEADS = 2
NSA_HPG = NSA_HEADS // NSA_KV_HEADS
NSA_WIDTH = NSA_HEADS * HEAD_DIM
CMP_BLOCK = 64
SEL_BLOCK = 64
TOP_N = 16
N_FORCED_LOCAL = 2
WINDOW = 512
RMS_EPS = 1e-6
NEG = -1e30
SCALE = HEAD_DIM ** -0.5
LOG2E = 1.4426950408889634
SPLIT_SIZES = (FOX_WIDTH, 2 * FOX_WIDTH, FOX_HEADS, FOX_WIDTH,
               NSA_WIDTH, 4 * NSA_KV_HEADS * HEAD_DIM, 2 * NSA_KV_HEADS * HEAD_DIM, 3 * NSA_HEADS, NSA_WIDTH,
               2 * D_MODEL)
SMALL_W = 128
GATE_COL = FOX_HEADS
ALIBI_START = 2.0 ** (-8.0 / NSA_HEADS)

VMEM_LIMIT = 56 * 1024 * 1024

NT_DIMS = (((1,), (1,)), ((), ()))
TN_DIMS = (((0,), (0,)), ((), ()))


def _params(*sem):
    return pltpu.CompilerParams(dimension_semantics=sem, vmem_limit_bytes=VMEM_LIMIT)


def _sigmoid(x):
    return 1.0 / (1.0 + jnp.exp(-x))


def _iota(shape, dim):
    return lax.broadcasted_iota(jnp.int32, shape, dim)


def _div_pow2(x, n):
    assert n & (n - 1) == 0
    return x >> (n.bit_length() - 1)


def _rmsnorm_body(x_ref, g_ref, o_ref):
    x = x_ref[...]
    ms = jnp.mean(x * x, axis=-1, keepdims=True)
    o_ref[...] = (x * lax.rsqrt(ms + RMS_EPS) * g_ref[...]).astype(o_ref.dtype)


def _rmsnorm(x2d, g, out_dtype, tm):
    m, d = x2d.shape
    return pl.pallas_call(
        _rmsnorm_body,
        grid=(m // tm,),
        in_specs=[pl.BlockSpec((tm, d), lambda i: (i, 0)), pl.BlockSpec((1, d), lambda i: (0, 0))],
        out_specs=pl.BlockSpec((tm, d), lambda i: (i, 0)),
        out_shape=jax.ShapeDtypeStruct((m, d), out_dtype),
        compiler_params=_params("parallel"),
        name="rmsnorm",
    )(x2d, g.reshape(1, d))


def _mm_body(h_ref, w_ref, o_ref, *, act):
    acc = jnp.dot(h_ref[...], w_ref[...], preferred_element_type=F32)
    if act == "sigmoid":
        acc = _sigmoid(acc)
    o_ref[...] = acc


def _mm(h, w, act=None, tn=512):
    m, k = h.shape
    n = w.shape[1]
    tm = min(m, 1024)
    tn = min(tn, n)
    return pl.pallas_call(
        functools.partial(_mm_body, act=act),
        grid=(m // tm, n // tn),
        in_specs=[pl.BlockSpec((tm, k), lambda i, j: (i, 0)), pl.BlockSpec((k, tn), lambda i, j: (0, j))],
        out_specs=pl.BlockSpec((tm, tn), lambda i, j: (i, j)),
        out_shape=jax.ShapeDtypeStruct((m, n), F32),
        compiler_params=_params("parallel", "arbitrary"),
        name="proj_" + (act or "lin"),
    )(h, w)


def _mm_small_body(h_ref, w_ref, b_ref, o_ref):
    v = jnp.dot(h_ref[...], w_ref[...], preferred_element_type=F32) + b_ref[...]
    col = _iota(v.shape, 1)
    logsig = jnp.minimum(v, 0.0) - jnp.log1p(jnp.exp(-jnp.abs(v)))
    o_ref[...] = jnp.where(col < FOX_HEADS, logsig, _sigmoid(v))


def _mm_small(h, w, b):
    m, k = h.shape
    tm = min(m, 1024)
    return pl.pallas_call(
        _mm_small_body,
        grid=(m // tm,),
        in_specs=[pl.BlockSpec((tm, k), lambda i: (i, 0)), pl.BlockSpec((k, SMALL_W), lambda i: (0, 0)),
                  pl.BlockSpec((1, SMALL_W), lambda i: (0, 0))],
        out_specs=pl.BlockSpec((tm, SMALL_W), lambda i: (i, 0)),
        out_shape=jax.ShapeDtypeStruct((m, SMALL_W), F32),
        compiler_params=_params("parallel"),
        name="proj_small",
    )(h, w, b)


def _mm_slab_body(h_ref, w_ref, o_ref, *, nslab):
    acc = jnp.dot(h_ref[...], w_ref[...], preferred_element_type=F32)
    for c in range(nslab):
        o_ref[c] = acc[:, c * HEAD_DIM:(c + 1) * HEAD_DIM]


def _mm_slab(h, w):
    m, k = h.shape
    n = w.shape[1]
    nslab = n // HEAD_DIM
    tm = min(m, 1024)
    return pl.pallas_call(
        functools.partial(_mm_slab_body, nslab=nslab),
        grid=(m // tm,),
        in_specs=[pl.BlockSpec((tm, k), lambda i: (i, 0)), pl.BlockSpec((k, n), lambda i: (0, 0))],
        out_specs=pl.BlockSpec((nslab, tm, HEAD_DIM), lambda i: (0, i, 0)),
        out_shape=jax.ShapeDtypeStruct((nslab, m, HEAD_DIM), F32),
        compiler_params=_params("parallel"),
        name="proj_slab",
    )(h, w)


def _compress_body(x_ref, pos_ref, w1_ref, w2_ref, o_ref):
    x = x_ref[0] + pos_ref[0]
    h1 = jnp.dot(x.astype(BF16), w1_ref[0], preferred_element_type=F32)
    a = h1 * _sigmoid(h1)
    o_ref[0] = jnp.dot(a.astype(BF16), w2_ref[0], preferred_element_type=F32)


def _compress(xc, pos, w1, w2):
    nslab, r, kk = xc.shape
    tr = min(r, 256)
    kind = lambda c, i: (c // NSA_KV_HEADS, 0, 0)
    return pl.pallas_call(
        _compress_body,
        grid=(nslab, r // tr),
        in_specs=[pl.BlockSpec((1, tr, kk), lambda c, i: (c, i, 0)),
                  pl.BlockSpec((1, 1, kk), kind),
                  pl.BlockSpec((1, kk, HEAD_DIM), kind),
                  pl.BlockSpec((1, HEAD_DIM, HEAD_DIM), kind)],
        out_specs=pl.BlockSpec((1, tr, HEAD_DIM), lambda c, i: (c, i, 0)),
        out_shape=jax.ShapeDtypeStruct((nslab, r, HEAD_DIM), F32),
        compiler_params=_params("parallel", "parallel"),
        name="nsa_compress",
    )(xc, pos, w1, w2)


def _cumsum_body(lf_ref, ccol_ref, crow_ref, *, t):
    tri = (_iota((128, 128), 0) >= _iota((128, 128), 1)).astype(F32)
    carry = jnp.zeros((1, SMALL_W), F32)
    for ch in range(t // 128):
        rows = slice(ch * 128, (ch + 1) * 128)
        cs = jnp.dot(tri, lf_ref[rows, :], precision=HIGHEST, preferred_element_type=F32) + carry
        ccol_ref[rows, :] = cs
        crow_ref[0, :, rows] = cs.T[0:FOX_HEADS, :]
        carry = cs[127:128, :]


def _cumsum_logf(small, b, t):
    return pl.pallas_call(
        functools.partial(_cumsum_body, t=t),
        grid=(b,),
        in_specs=[pl.BlockSpec((t, SMALL_W), lambda i: (i, 0))],
        out_specs=[pl.BlockSpec((t, SMALL_W), lambda i: (i, 0)),
                   pl.BlockSpec((1, FOX_HEADS, t), lambda i: (i, 0, 0))],
        out_shape=[jax.ShapeDtypeStruct((b * t, SMALL_W), F32), jax.ShapeDtypeStruct((b, FOX_HEADS, t), F32)],
        compiler_params=_params("parallel"),
        name="fox_cumsum",
    )(small)


def _flash_loop(q, k_ref, v_ref, kt_lo, kt_hi, tk, score_fn, select, s_sc, m_sc, l_sc, acc_sc, kt_mid=None):
    m_sc[...] = jnp.full(m_sc.shape, NEG, F32)
    l_sc[...] = jnp.zeros(l_sc.shape, F32)
    acc_sc[...] = jnp.zeros(acc_sc.shape, F32)

    def raw_scores(kt):
        k0 = pl.multiple_of(kt * tk, tk)
        return lax.dot_general(q, k_ref[pl.ds(k0, tk), :].astype(BF16), NT_DIMS, preferred_element_type=F32)

    s_sc[kt_lo & 1] = raw_scores(kt_lo)

    def make_body(masked):
        def body(kt, carry):
            k0 = pl.multiple_of(kt * tk, tk)
            s = s_sc[kt & 1]
            s_sc[(kt + 1) & 1] = raw_scores(jnp.minimum(kt + 1, kt_hi - 1))
            vb = v_ref[pl.ds(k0, tk), :].astype(BF16)
            s, row_c, mask = score_fn(s, kt, k0, masked)
            if mask is not None:
                s = select(mask, s, NEG)
            m_old = m_sc[...]
            m_new = jnp.maximum(m_old, jnp.max(s, axis=-1, keepdims=True) + row_c)
            alpha = jnp.exp2(m_old - m_new)
            p = jnp.exp2(s - (m_new - row_c))
            if mask is not None:
                p = select(mask, p, 0.0)
            l_sc[...] = alpha * l_sc[...] + jnp.sum(p, axis=-1, keepdims=True)
            acc_sc[...] = alpha * acc_sc[...] + jnp.dot(p.astype(BF16), vb, preferred_element_type=F32)
            m_sc[...] = m_new
            return carry
        return body

    if kt_mid is None:
        kt_mid = kt_lo
    else:
        lax.fori_loop(kt_lo, kt_mid, make_body(False), 0)
    lax.fori_loop(kt_mid, kt_hi, make_body(True), 0)
    return acc_sc[...] / jnp.maximum(l_sc[...], 1e-30)


def _fox_body(q_ref, k_ref, v_ref, ccol_ref, crow_ref, o_ref, s_sc, m_sc, l_sc, acc_sc, *, tq, tk):
    h = pl.program_id(1)
    q0 = pl.program_id(2) * tq
    q = (q_ref[...] * (SCALE * LOG2E)).astype(BF16)
    lane = _iota((tq, SMALL_W), 1)
    cq = jnp.sum(jnp.where(lane == h, ccol_ref[...], 0.0), axis=-1, keepdims=True) * LOG2E
    qpos = q0 + _iota((tq, 1), 0)
    kloc = _iota((1, tk), 1)

    def score_fn(s, kt, k0, masked):
        ck = crow_ref[0, 0, kt] * LOG2E
        return s - ck, cq, ((k0 + kloc) <= qpos) if masked else None

    assert tq == tk
    kt_diag = q0 // tk
    o_ref[...] = _flash_loop(q, k_ref, v_ref, 0, kt_diag + 1, tk, score_fn, jnp.where, s_sc, m_sc, l_sc, acc_sc,
                             kt_mid=kt_diag)


def _fox_prompt(fq, fkv, ccol, crow, b, t):
    tq = tk = 512
    nq, nkt = t // tq, t // tk
    crow5 = crow.reshape(b, FOX_HEADS, nkt, 1, tk)
    return pl.pallas_call(
        functools.partial(_fox_body, tq=tq, tk=tk),
        grid=(b, FOX_HEADS, nq),
        in_specs=[pl.BlockSpec((tq, HEAD_DIM), lambda i, h, j: (i * nq + j, h)),
                  pl.BlockSpec((t, HEAD_DIM), lambda i, h, j: (i, h)),
                  pl.BlockSpec((t, HEAD_DIM), lambda i, h, j: (i, FOX_HEADS + h)),
                  pl.BlockSpec((tq, SMALL_W), lambda i, h, j: (i * nq + j, 0)),
                  pl.BlockSpec((1, 1, nkt, 1, tk), lambda i, h, j: (i, h, 0, 0, 0))],
        out_specs=pl.BlockSpec((tq, HEAD_DIM), lambda i, h, j: (i * nq + j, h)),
        out_shape=jax.ShapeDtypeStruct((b * t, FOX_WIDTH), F32),
        scratch_shapes=[pltpu.VMEM((2, tq, tk), F32), pltpu.VMEM((tq, 1), F32), pltpu.VMEM((tq, 1), F32),
                        pltpu.VMEM((tq, HEAD_DIM), F32)],
        compiler_params=_params("parallel", "parallel", "arbitrary"),
        name="fox_prompt",
    )(fq, fkv, fkv, ccol, crow5)


def _masked_softmax(s, mask, axis):
    s = jnp.where(mask, s, NEG)
    m = jnp.max(s, axis=axis, keepdims=True)
    e = jnp.where(mask, jnp.exp(s - m), 0.0)
    return e / jnp.maximum(jnp.sum(e, axis=axis, keepdims=True), 1e-30)


def _group_slope_scale(g):
    scale = jnp.float32(1.0)
    for gg in range(1, NSA_KV_HEADS):
        scale = jnp.where(g == gg, jnp.float32(ALIBI_START ** (NSA_HPG * gg)), scale)
    return scale


def _head_slopes(head_idx, n):
    out = jnp.zeros(head_idx.shape, F32)
    for hh in range(n):
        out = jnp.where(head_idx == hh, jnp.float32(ALIBI_START ** (hh + 1)), out)
    return out


def _nsa_body(q_ref, kc_ref, vc_ref, ks_ref, vs_ref, kw_ref, vw_ref, ocmp_ref, oslc_ref, owin_ref,
              s_sc, m_sc, l_sc, acc_sc, pick_sc, *, tq, tk, n_blk):
    g = pl.program_id(1)
    q0 = pl.program_id(2) * tq
    rows = NSA_HPG * tq
    lg_tq = tq.bit_length() - 1
    qf = q_ref[...]
    q4f = jnp.concatenate([qf[:, j * HEAD_DIM:(j + 1) * HEAD_DIM] for j in range(NSA_HPG)], axis=0)
    q4 = q4f.astype(BF16)
    q4s = (q4f * (SCALE * LOG2E)).astype(BF16)
    gscale = _group_slope_scale(g)

    def unstack(o):
        return jnp.concatenate([o[j * tq:(j + 1) * tq] for j in range(NSA_HPG)], axis=1)

    ri = _iota((rows, 1), 0)
    qpos = q0 + (ri & (tq - 1))
    slope = _head_slopes(ri >> lg_tq, NSA_HPG) * gscale
    kc = kc_ref[0].astype(BF16)
    vc = vc_ref[0].astype(BF16)
    cend = (_iota((1, n_blk), 1) + 1) * CMP_BLOCK - 1
    dc = qpos - cend
    sc = lax.dot_general(q4, kc, NT_DIMS, preferred_element_type=F32) * SCALE - slope * dc.astype(F32)
    p = _masked_softmax(sc, dc >= 0, -1)
    ocmp_ref[...] = unstack(jnp.dot(p.astype(BF16), vc, preferred_element_type=F32))

    ci = _iota((1, rows), 1)
    qpos_t = q0 + (ci & (tq - 1))
    slope_t = _head_slopes(ci >> lg_tq, NSA_HPG) * gscale
    blk = _iota((n_blk, 1), 0)
    dct = qpos_t - ((blk + 1) * CMP_BLOCK - 1)
    sct = lax.dot_general(kc, q4, NT_DIMS, preferred_element_type=F32) * SCALE - slope_t * dct.astype(F32)
    pt = _masked_softmax(sct, dct >= 0, 0)
    imp = pt[:, 0:tq]
    for j in range(1, NSA_HPG):
        imp = imp + pt[:, j * tq:(j + 1) * tq]
    own = _div_pow2(q0 + _iota((1, tq), 1), SEL_BLOCK)
    forced = ((blk <= own) & (blk > own - N_FORCED_LOCAL)) | (blk == 0)
    imp = jnp.where(forced, 1e9, jnp.where(blk > own, -1e9, imp))
    rank = jnp.zeros((n_blk, tq), F32)
    for mm in range(n_blk):
        row = imp[mm:mm + 1, :]
        rank = rank + jnp.where(blk > mm, (row >= imp).astype(F32), (row > imp).astype(F32))
    sel_b = jnp.where(rank < TOP_N, 1.0, 0.0).astype(BF16)

    kloc = _iota((1, tk), 1)
    qpos_q = q0 + _iota((tq, 1), 0)
    slope2 = slope * LOG2E
    bias0 = slope2 * (qpos - kloc).astype(F32)
    e_row = _iota((n_blk, tk), 0)
    e_blk = _div_pow2(_iota((n_blk, tk), 1), SEL_BLOCK)
    for kk in range(n_blk * SEL_BLOCK // tk):
        expand = (e_row == kk * (tk // SEL_BLOCK) + e_blk).astype(BF16)
        pick_sc[kk] = lax.dot_general(sel_b, expand, TN_DIMS, preferred_element_type=F32)

    def alibi(s, k0):
        return s - bias0, slope2 * k0.astype(F32)

    def select(mask_q, x, fill):
        return jnp.where(mask_q[None], x.reshape(NSA_HPG, tq, tk), fill).reshape(rows, tk)

    def slc_score(s, kt, k0, masked):
        mask_q = (pick_sc[kt] > 0.5) & ((k0 + kloc) <= qpos_q)
        return alibi(s, k0) + (mask_q,)

    def win_score(s, kt, k0, masked):
        dist = qpos_q - (k0 + kloc)
        mask_q = (dist >= 0) & (dist < WINDOW)
        return alibi(s, k0) + (mask_q,)

    kt_hi = q0 // tk + 1
    oslc_ref[...] = unstack(_flash_loop(q4s, ks_ref, vs_ref, 0, kt_hi, tk, slc_score, select,
                                        s_sc, m_sc, l_sc, acc_sc))
    kt_lo = jnp.maximum(q0 - (WINDOW - 1), 0) // tk
    owin_ref[...] = unstack(_flash_loop(q4s, kw_ref, vw_ref, kt_lo, kt_hi, tk, win_score, select,
                                        s_sc, m_sc, l_sc, acc_sc))


def _nsa_prompt(nq, cmpkv, nkv, nwkv, b, t):
    tq, tk = 128, 512
    nqt = t // tq
    n_blk = t // CMP_BLOCK
    gw = NSA_HPG * HEAD_DIM
    rows = NSA_HPG * tq
    qmap = lambda i, g, j: (i * nqt + j, g)
    col = lambda c: (lambda i, g, j: (i, c * NSA_KV_HEADS + g))
    slab = lambda c: (lambda i, g, j: (c * NSA_KV_HEADS + g, i, 0))
    out = jax.ShapeDtypeStruct((b * t, NSA_WIDTH), F32)
    return pl.pallas_call(
        functools.partial(_nsa_body, tq=tq, tk=tk, n_blk=n_blk),
        grid=(b, NSA_KV_HEADS, nqt),
        in_specs=[pl.BlockSpec((tq, gw), qmap),
                  pl.BlockSpec((1, n_blk, HEAD_DIM), slab(0)),
                  pl.BlockSpec((1, n_blk, HEAD_DIM), slab(1)),
                  pl.BlockSpec((t, HEAD_DIM), col(2)),
                  pl.BlockSpec((t, HEAD_DIM), col(3)),
                  pl.BlockSpec((t, HEAD_DIM), col(0)),
                  pl.BlockSpec((t, HEAD_DIM), col(1))],
        out_specs=[pl.BlockSpec((tq, gw), qmap)] * 3,
        out_shape=[out, out, out],
        scratch_shapes=[pltpu.VMEM((2, rows, tk), F32), pltpu.VMEM((rows, 1), F32), pltpu.VMEM((rows, 1), F32),
                        pltpu.VMEM((rows, HEAD_DIM), F32), pltpu.VMEM((t // tk, tq, tk), F32)],
        compiler_params=_params("parallel", "parallel", "arbitrary"),
        name="nsa_prompt",
    )(nq, cmpkv, cmpkv, nkv, nkv, nwkv, nwkv)


def _merge_body(x_ref, of_ref, fg_ref, oc_ref, os_ref, ow_ref, ng_ref, sm_ref, mg_ref, wbf_ref, wbn_ref, wo_ref,
                gf_ref, o_ref, *, final):
    fg = fg_ref[...]
    ya = jnp.dot((of_ref[...] * (fg * _sigmoid(fg))).astype(BF16), wbf_ref[...], preferred_element_type=F32)
    sm = sm_ref[...]
    parts = []
    for hd in range(NSA_HEADS):
        cols = slice(hd * HEAD_DIM, (hd + 1) * HEAD_DIM)
        gate = lambda br: sm[:, GATE_COL + br * NSA_HEADS + hd:GATE_COL + br * NSA_HEADS + hd + 1]
        parts.append(gate(0) * oc_ref[:, cols] + gate(1) * os_ref[:, cols] + gate(2) * ow_ref[:, cols])
    ng = ng_ref[...]
    on = jnp.concatenate(parts, axis=1) * (ng * _sigmoid(ng))
    yb = jnp.dot(on.astype(BF16), wbn_ref[...], preferred_element_type=F32)
    mix = mg_ref[:, 0:D_MODEL] * ya + mg_ref[:, D_MODEL:2 * D_MODEL] * yb
    xn = x_ref[...] + jnp.dot(mix.astype(BF16), wo_ref[...], preferred_element_type=F32)
    if final:
        ms = jnp.mean(xn * xn, axis=-1, keepdims=True)
        xn = xn * lax.rsqrt(ms + RMS_EPS) * gf_ref[...]
    o_ref[...] = xn


def _merge(x, o_fox, fgate, o_cmp, o_slc, o_win, ngate, small, mg, w_bf, w_bn, w_o, g_final, final):
    m = x.shape[0]
    tm = min(m, 128)
    row = lambda w: pl.BlockSpec((tm, w), lambda i: (i, 0))
    whole = lambda a: pl.BlockSpec(a.shape, lambda i: (0, 0), pipeline_mode=pl.Buffered(1))
    gf = g_final.reshape(1, D_MODEL)
    return pl.pallas_call(
        functools.partial(_merge_body, final=final),
        grid=(m // tm,),
        in_specs=[row(D_MODEL), row(FOX_WIDTH), row(FOX_WIDTH), row(NSA_WIDTH), row(NSA_WIDTH), row(NSA_WIDTH),
                  row(NSA_WIDTH), row(SMALL_W), row(2 * D_MODEL), whole(w_bf), whole(w_bn), whole(w_o), whole(gf)],
        out_specs=row(D_MODEL),
        out_shape=jax.ShapeDtypeStruct((m, D_MODEL), F32),
        compiler_params=_params("parallel"),
        name="merge_final" if final else "merge",
    )(x, o_fox, fgate, o_cmp, o_slc, o_win, ngate, small, mg, w_bf, w_bn, w_o, gf)


ROWS_PER_TOKEN = 8
assert ROWS_PER_TOKEN == FOX_HEADS == NSA_HEADS == 4 * NSA_KV_HEADS
CMP_CHUNK = 8
assert CMP_BLOCK % CMP_CHUNK == 0


def _rep_rows(col, n):
    return jnp.concatenate([col] * n, axis=0)


def _paged_softmax(val, mask, s_new, n_pages):
    val = jnp.where(mask, val, NEG)
    m_rows = jnp.max(val, axis=-1, keepdims=True)
    m = jnp.maximum(jnp.max(m_rows.reshape(n_pages, ROWS_PER_TOKEN, 1), axis=0), s_new)
    e = jnp.where(mask, jnp.exp(val - _rep_rows(m, n_pages)), 0.0)
    e_new = jnp.exp(s_new - m)
    l_rows = jnp.sum(e, axis=-1, keepdims=True)
    inv = 1.0 / jnp.maximum(jnp.sum(l_rows.reshape(n_pages, ROWS_PER_TOKEN, 1), axis=0) + e_new, 1e-30)
    return e * _rep_rows(inv, n_pages), e_new * inv


def _fox_dec_body(pt_ref, *refs, n_pages, page):
    kv = refs[:n_pages]
    lt = refs[n_pages:2 * n_pages]
    q_ref, kvn_ref, lfn_ref, o_ref = refs[2 * n_pages:]
    hh = ROWS_PER_TOKEN
    rows = n_pages * hh
    lanes = page * hh
    q = q_ref[...]
    qb = q.astype(BF16)
    k_new = kvn_ref[0]
    v_new = kvn_ref[1]
    s_new = jnp.sum(q * k_new, axis=-1, keepdims=True) * SCALE

    lt_all = jnp.concatenate([lt[j][...] for j in range(n_pages)], axis=0)
    later = (_iota((page, page), 0) > _iota((page, page), 1)).astype(F32)
    within = jnp.dot(lt_all, later, precision=HIGHEST, preferred_element_type=F32)
    tot = jnp.broadcast_to(jnp.sum(lt_all, axis=-1, keepdims=True), (rows, page))
    rr = _iota((rows, rows), 0)
    cc = _iota((rows, rows), 1)
    after = (((cc & (hh - 1)) == (rr & (hh - 1))) & (cc > rr)).astype(F32)
    decay = (within + jnp.dot(after, tot, precision=HIGHEST, preferred_element_type=F32)
             + _rep_rows(lfn_ref[...], n_pages))
    hi = decay.astype(BF16)
    r1 = decay - hi.astype(F32)
    mid = r1.astype(BF16)
    lo = (r1 - mid.astype(F32)).astype(BF16)
    spread = (_iota((page, lanes), 0) == _div_pow2(_iota((page, lanes), 1), hh)).astype(BF16)
    d3 = jnp.dot(jnp.concatenate([hi, mid, lo], axis=0), spread, preferred_element_type=F32)
    decay_l = d3[0:rows] + d3[rows:2 * rows] + d3[2 * rows:3 * rows]

    s_all = jnp.concatenate(
        [lax.dot_general(qb, kv[j][:, 0].reshape(lanes, HEAD_DIM).astype(BF16), NT_DIMS, preferred_element_type=F32)
         for j in range(n_pages)], axis=0)
    own_head = (_iota((rows, 1), 0) & (hh - 1)) == (_iota((1, lanes), 1) & (hh - 1))
    p, p_new = _paged_softmax(s_all * SCALE + decay_l, own_head, s_new, n_pages)
    pb = p.astype(BF16)
    acc = p_new * v_new
    for j in range(n_pages):
        vb = kv[j][:, 1].reshape(lanes, HEAD_DIM).astype(BF16)
        acc = acc + jnp.dot(pb[j * hh:(j + 1) * hh], vb, preferred_element_type=F32)
    o_ref[...] = acc


def _fox_decode(cache_kv, cache_lf_t, page_table, layer, fq, fkv, small):
    s, n_pages = page_table.shape
    page = cache_kv.shape[2]
    hh = ROWS_PER_TOKEN
    kv_specs = [pl.BlockSpec((None, None, page, 2, hh, HEAD_DIM), lambda i, pt, j=j: (layer, pt[i, j], 0, 0, 0, 0))
                for j in range(n_pages)]
    lt_specs = [pl.BlockSpec((None, None, hh, page), lambda i, pt, j=j: (layer, pt[i, j], 0, 0))
                for j in range(n_pages)]
    out = pl.pallas_call(
        functools.partial(_fox_dec_body, n_pages=n_pages, page=page),
        grid_spec=pltpu.PrefetchScalarGridSpec(
            num_scalar_prefetch=1,
            grid=(s,),
            in_specs=kv_specs + lt_specs
            + [pl.BlockSpec((None, hh, HEAD_DIM), lambda i, pt: (i, 0, 0)),
               pl.BlockSpec((None, 2, hh, HEAD_DIM), lambda i, pt: (i, 0, 0, 0)),
               pl.BlockSpec((None, hh, 1), lambda i, pt: (i, 0, 0))],
            out_specs=pl.BlockSpec((None, hh, HEAD_DIM), lambda i, pt: (i, 0, 0)),
        ),
        out_shape=jax.ShapeDtypeStruct((s, hh, HEAD_DIM), F32),
        compiler_params=_params("arbitrary"),
        name="fox_decode",
    )(page_table, *([cache_kv] * n_pages), *([cache_lf_t] * n_pages),
      fq.reshape(s, hh, HEAD_DIM), fkv.reshape(s, 2, hh, HEAD_DIM), small[:, 0:hh].reshape(s, hh, 1))
    return out.reshape(s, FOX_WIDTH)


def _nsa_dec_body(pt_ref, *refs, n_pages, page, win_buf):
    pg = refs[:n_pages]
    (win_ref, q_ref, kvn_ref, wn_ref, pos_ref, w1_ref, w2_ref) = refs[n_pages:n_pages + 7]
    ocmp_ref, oslc_ref, owin_ref, wnext_ref, acc_sc, cm_sc = refs[-6:]
    nk = ROWS_PER_TOKEN
    gg_n = NSA_KV_HEADS
    past = n_pages * page
    n_blk = past // CMP_BLOCK
    own = past // SEL_BLOCK
    bpp = page // CMP_BLOCK
    q = q_ref[...]
    qb = q.astype(BF16)
    head = _iota((NSA_HEADS, 1), 0)
    grp = _div_pow2(head, NSA_HPG)
    slope = _head_slopes(head, NSA_HEADS)

    def group_row(a, first):
        out = jnp.zeros((NSA_HEADS, HEAD_DIM), F32)
        for gg in range(gg_n):
            out = jnp.where(grp == gg, jnp.broadcast_to(a[first + gg:first + gg + 1, :], (NSA_HEADS, HEAD_DIM)), out)
        return out

    acc_sc[...] = jnp.zeros(acc_sc.shape, F32)

    def cbody(ic, carry):
        xs = []
        for ii in range(CMP_CHUNK):
            i = ic * CMP_CHUNK + ii
            pos_i = pos_ref[i]
            tiles = []
            for j in range(n_pages):
                for hb in range(bpp):
                    r0 = pl.multiple_of((hb * CMP_BLOCK + i) * nk, nk)
                    tiles.append(pg[j][pl.ds(r0, nk), :] + pos_i)
            xs.append(jnp.concatenate(tiles, axis=0).astype(BF16))
        x = jnp.concatenate(xs, axis=1)
        acc_sc[...] += jnp.dot(x, w1_ref[ic], preferred_element_type=F32)
        return carry

    lax.fori_loop(0, CMP_BLOCK // CMP_CHUNK, cbody, 0)
    krow = _iota((n_blk * nk, 1), 0) & (nk - 1)
    h1 = acc_sc[...]
    h1 = jnp.where(krow < gg_n, h1[:, 0:HEAD_DIM], h1[:, HEAD_DIM:2 * HEAD_DIM])
    c2 = jnp.dot((h1 * _sigmoid(h1)).astype(BF16), w2_ref[...], preferred_element_type=F32)
    cm = jnp.where(krow < gg_n, c2[:, 0:HEAD_DIM], c2[:, HEAD_DIM:2 * HEAD_DIM])
    cm_sc[...] = cm

    def by_group(fn):
        out = fn(0)
        for gg in range(1, gg_n):
            out = jnp.where(grp == gg, fn(gg), out)
        return out

    def scores(rows_fn, first):
        return by_group(lambda gg: lax.dot_general(qb, rows_fn(first + gg), NT_DIMS, preferred_element_type=F32))

    def weighted(pb, rows_fn, first):
        return by_group(lambda gg: jnp.dot(pb, rows_fn(first + gg), preferred_element_type=F32))

    def cmp_rows(k):
        return cm_sc[pl.ds(k, n_blk, stride=nk), :].astype(BF16)

    nb = _iota((1, n_blk), 1)
    dc = past - ((nb + 1) * CMP_BLOCK - 1)
    sc = scores(cmp_rows, 0) * SCALE - slope * dc.astype(F32)
    p = _masked_softmax(sc, dc >= 0, -1)
    ocmp_ref[...] = weighted(p.astype(BF16), cmp_rows, gg_n)

    forced = (nb == 0) | ((nb <= own) & (nb > own - N_FORCED_LOCAL))
    mrow = _iota((n_blk, n_blk), 0)
    ncol = _iota((n_blk, n_blk), 1)
    sel = jnp.zeros((NSA_HEADS, n_blk), F32)
    for gg in range(gg_n):
        imp = jnp.sum(jnp.where(grp == gg, p, 0.0), axis=0, keepdims=True)
        imp = jnp.where(forced, 1e9, imp)
        imp_b = jnp.broadcast_to(imp, (n_blk, n_blk))
        imp_col = jnp.sum(jnp.where(mrow == ncol, imp_b, 0.0), axis=-1, keepdims=True)
        beats = jnp.where(mrow < ncol, (imp_col >= imp_b).astype(F32), (imp_col > imp_b).astype(F32))
        rank = jnp.sum(beats, axis=0, keepdims=True) + jnp.where(forced, 0.0, 1.0)
        sel = jnp.where(grp == gg, jnp.where(rank < TOP_N, 1.0, 0.0), sel)
    expand = (_iota((n_blk, past), 0) == _div_pow2(_iota((n_blk, past), 1), SEL_BLOCK)).astype(BF16)
    sel_tok = jnp.dot(sel.astype(BF16), expand, preferred_element_type=F32) > 0.5

    def page_rows(k):
        return jnp.concatenate([pg[j][pl.ds(k, page, stride=nk), :] for j in range(n_pages)], axis=0).astype(BF16)

    dist = (past - _iota((1, past), 1)).astype(F32)
    val = scores(page_rows, 2 * gg_n) * SCALE - slope * dist
    kvn = kvn_ref[...]
    s_new = jnp.sum(q * group_row(kvn, 2 * gg_n), axis=-1, keepdims=True) * SCALE
    ps, ps_new = _paged_softmax(val, sel_tok, s_new, 1)
    oslc_ref[...] = ps_new * group_row(kvn, 3 * gg_n) + weighted(ps.astype(BF16), page_rows, 3 * gg_n)

    wr = 2 * gg_n

    def win_rows_of(k):
        return win_ref[pl.ds(k, win_buf, stride=wr), :].astype(BF16)

    wdist = win_buf - _iota((1, win_buf), 1)
    sw = scores(win_rows_of, 0) * SCALE - slope * wdist.astype(F32)
    wmask = (wdist < WINDOW) & (past - wdist >= 0)
    wn = wn_ref[...]
    sw_new = jnp.sum(q * group_row(wn, 0), axis=-1, keepdims=True) * SCALE
    pw, pw_new = _paged_softmax(sw, wmask, sw_new, 1)
    owin_ref[...] = pw_new * group_row(wn, gg_n) + weighted(pw.astype(BF16), win_rows_of, gg_n)

    n_rows = win_buf * wr
    shifted = pltpu.roll(win_ref[...], n_rows - wr, axis=0)
    wnext_ref[...] = shifted
    tail = _iota((nk, HEAD_DIM), 0) >= nk - wr
    wnext_ref[n_rows - nk:n_rows, :] = jnp.where(tail, jnp.concatenate([wn] * (nk // wr), axis=0),
                                                 shifted[n_rows - nk:n_rows])


def _nsa_decode(cache_rows, win_rows, page_table, layer, nq, nkv, nwkv, pos8, w1cat, w2cat, win_next):
    s, n_pages = page_table.shape
    depth = win_rows.shape[0]
    nk = ROWS_PER_TOKEN
    page = cache_rows.shape[2] // nk
    wr = 2 * NSA_KV_HEADS
    win_buf = win_rows.shape[2] // wr
    n_blk = n_pages * page // CMP_BLOCK
    pg_specs = [pl.BlockSpec((None, None, page * nk, HEAD_DIM), lambda i, pt, j=j: (layer, pt[i, j], 0, 0))
                for j in range(n_pages)]
    whole = lambda a: pl.BlockSpec(a.shape, lambda i, pt: (0,) * a.ndim, pipeline_mode=pl.Buffered(1))
    head_rows = lambda n: pl.BlockSpec((None, n, HEAD_DIM), lambda i, pt: (i, 0, 0))
    out = jax.ShapeDtypeStruct((s, NSA_HEADS, HEAD_DIM), F32)
    win_spec = pl.BlockSpec((None, None, win_buf * wr, HEAD_DIM), lambda i, pt: (layer, i, 0, 0))
    args = [page_table, *([cache_rows] * n_pages), win_rows,
            nq.reshape(s, NSA_HEADS, HEAD_DIM), nkv.reshape(s, nk, HEAD_DIM), nwkv.reshape(s, wr, HEAD_DIM),
            pos8, w1cat, w2cat]
    in_specs = pg_specs + [win_spec, head_rows(NSA_HEADS), head_rows(nk), head_rows(wr),
                           whole(pos8), whole(w1cat), whole(w2cat)]
    aliases = {}
    if win_next is not None:
        aliases = {len(args): 3}
        args.append(win_next)
        in_specs.append(pl.BlockSpec(memory_space=pl.ANY))
    outs = pl.pallas_call(
        functools.partial(_nsa_dec_body, n_pages=n_pages, page=page, win_buf=win_buf),
        grid_spec=pltpu.PrefetchScalarGridSpec(
            num_scalar_prefetch=1,
            grid=(s,),
            in_specs=in_specs,
            out_specs=[head_rows(NSA_HEADS)] * 3 + [win_spec],
            scratch_shapes=[pltpu.VMEM((n_blk * nk, 2 * HEAD_DIM), F32), pltpu.VMEM((n_blk * nk, HEAD_DIM), F32)],
        ),
        out_shape=[out, out, out, jax.ShapeDtypeStruct((depth, s, win_buf * wr, HEAD_DIM), F32)],
        input_output_aliases=aliases,
        compiler_params=_params("arbitrary"),
        name="nsa_decode",
    )(*args)
    return [o.reshape(s, NSA_WIDTH) for o in outs[:3]], outs[3]


def _decode_compress_weights(w_cmp1_l, w_cmp2_l, cmp_pos_l):
    gg_n = NSA_KV_HEADS
    w1cat = w_cmp1_l.reshape(2, CMP_BLOCK, HEAD_DIM, HEAD_DIM).transpose(1, 2, 0, 3)
    w1cat = w1cat.reshape(CMP_BLOCK // CMP_CHUNK, CMP_CHUNK * HEAD_DIM, 2 * HEAD_DIM).astype(BF16)
    w2cat = jnp.concatenate([w_cmp2_l[0], w_cmp2_l[1]], axis=1).astype(BF16)
    pos = jnp.repeat(cmp_pos_l.transpose(1, 0, 2), gg_n, axis=1)
    pos8 = jnp.concatenate([pos, jnp.zeros((CMP_BLOCK, ROWS_PER_TOKEN - 2 * gg_n, HEAD_DIM), F32)], axis=1)
    return pos8, w1cat, w2cat


def _split_weights(w_in_l, b_f_l):
    off = [int(v) for v in np.cumsum((0,) + SPLIT_SIZES)]
    cols = lambda i: w_in_l[:, off[i]:off[i + 1]]
    names = ("fq", "fkv", "ff", "fgate", "nq", "nkv", "nwkv", "nbg", "ngate", "mg")
    w = {n: cols(i) for i, n in enumerate(names)}
    pad = SMALL_W - FOX_HEADS - 3 * NSA_HEADS
    small = jnp.concatenate([w.pop("ff"), w.pop("nbg"), jnp.zeros((D_MODEL, pad), F32)], axis=1)
    out = {n: v.astype(BF16) for n, v in w.items()}
    out["small"] = small.astype(BF16)
    out["b_small"] = jnp.concatenate([b_f_l, jnp.zeros((SMALL_W - FOX_HEADS,), F32)]).reshape(1, SMALL_W)
    out["cmp"] = out["nkv"][:, 0:2 * NSA_KV_HEADS * HEAD_DIM]
    return out


def _project(x2d, g, w):
    h = _rmsnorm(x2d, g, BF16, min(x2d.shape[0], 512))
    pr = {n: _mm(h, w[n]) for n in ("fq", "fkv", "fgate", "nq", "nkv", "nwkv", "ngate")}
    pr["mg"] = _mm(h, w["mg"], act="sigmoid")
    pr["small"] = _mm_small(h, w["small"], w["b_small"])
    return h, pr


def kernel(x_prompt, x_sample, cache_fox_kv, cache_fox_logf, cache_nsa_kv, cache_nsa_win, page_table,
           norm_g, w_in, b_fox_f, w_cmp1, w_cmp2, cmp_pos, w_branch_fox, w_branch_nsa, w_out, final_norm_g):
    b, t, d = x_prompt.shape
    s, dec_seq, _ = x_sample.shape
    depth, n_phys, page = cache_fox_kv.shape[:3]
    n_pages = page_table.shape[1]
    win_buf = cache_nsa_win.shape[2]
    win_p = min(WINDOW, t)
    assert dec_seq == 1 and d == D_MODEL and t % 512 == 0 and (n_pages * page) % SEL_BLOCK == 0
    assert NSA_KV_HEADS == 2 and win_buf <= n_pages * page and page % CMP_BLOCK == 0

    cache_rows = cache_nsa_kv.reshape(depth, n_phys, page * ROWS_PER_TOKEN, HEAD_DIM)
    win_rows = cache_nsa_win.reshape(depth, s, win_buf * 2 * NSA_KV_HEADS, HEAD_DIM)
    cache_lf_t = jnp.swapaxes(cache_fox_logf, 2, 3)
    page_table = page_table.astype(jnp.int32)

    xp = x_prompt.reshape(b * t, d)
    xs = x_sample.reshape(s, d)
    outs = {k: [] for k in ("fkv_p", "logf_p", "nkv_p", "win_p", "fkv_s", "logf_s", "nkv_s")}
    win_next = None
    for l in range(depth):
        final = l == depth - 1
        w = _split_weights(w_in[l], b_fox_f[l])
        w_bf = w_branch_fox[l].astype(BF16)
        w_bn = w_branch_nsa[l].astype(BF16)
        w_o = w_out[l].astype(BF16)
        w1 = w_cmp1[l].astype(BF16)
        w2 = w_cmp2[l].astype(BF16)
        pos = cmp_pos[l].reshape(2, 1, CMP_BLOCK * HEAD_DIM)

        h, pr = _project(xp, norm_g[l], w)
        ccol, crow = _cumsum_logf(pr["small"], b, t)
        o_fox = _fox_prompt(pr["fq"], pr["fkv"], ccol, crow, b, t)
        xc = _mm_slab(h, w["cmp"]).reshape(2 * NSA_KV_HEADS, b * (t // CMP_BLOCK), CMP_BLOCK * HEAD_DIM)
        cmpkv = _compress(xc, pos, w1, w2)
        o_cmp, o_slc, o_win = _nsa_prompt(pr["nq"], cmpkv, pr["nkv"], pr["nwkv"], b, t)
        xp = _merge(xp, o_fox, pr["fgate"], o_cmp, o_slc, o_win, pr["ngate"], pr["small"], pr["mg"],
                    w_bf, w_bn, w_o, final_norm_g, final)
        outs["fkv_p"].append(pr["fkv"].reshape(b, t, 2, FOX_HEADS, HEAD_DIM))
        outs["logf_p"].append(pr["small"][:, 0:FOX_HEADS].reshape(b, t, FOX_HEADS))
        outs["nkv_p"].append(pr["nkv"].reshape(b, t, 4, NSA_KV_HEADS, HEAD_DIM))
        outs["win_p"].append(pr["nwkv"].reshape(b, t, 2, NSA_KV_HEADS, HEAD_DIM)[:, t - win_p:])

        _, ps = _project(xs, norm_g[l], w)
        o_fox = _fox_decode(cache_fox_kv, cache_lf_t, page_table, l, ps["fq"], ps["fkv"], ps["small"])
        (o_cmp, o_slc, o_win), win_next = _nsa_decode(
            cache_rows, win_rows, page_table, l, ps["nq"], ps["nkv"], ps["nwkv"],
            *_decode_compress_weights(w_cmp1[l], w_cmp2[l], cmp_pos[l]), win_next)
        xs = _merge(xs, o_fox, ps["fgate"], o_cmp, o_slc, o_win, ps["ngate"], ps["small"], ps["mg"],
                    w_bf, w_bn, w_o, final_norm_g, final)
        outs["fkv_s"].append(ps["fkv"].reshape(s, 1, 2, FOX_HEADS, HEAD_DIM))
        outs["logf_s"].append(ps["small"][:, 0:FOX_HEADS].reshape(s, 1, FOX_HEADS))
        outs["nkv_s"].append(ps["nkv"].reshape(s, 1, 4, NSA_KV_HEADS, HEAD_DIM))

    st = {k: jnp.stack(v, axis=0) for k, v in outs.items()}
    win_s = win_next.reshape(cache_nsa_win.shape)
    return (xp.reshape(b, t, d), xs.reshape(s, 1, d), st["fkv_p"], st["logf_p"], st["nkv_p"], st["win_p"],
            st["fkv_s"], st["logf_s"], st["nkv_s"], win_s)
```

```python
import functools

import numpy as np
import jax
import jax.numpy as jnp
from jax import lax
from jax.experimental import pallas as pl
from jax.experimental.pallas import tpu as pltpu

F32 = jnp.float32
BF16 = jnp.bfloat16
HIGHEST = lax.Precision.HIGHEST

D_MODEL = 2048
HEAD_DIM = 128
FOX_HEADS = 8
FOX_WIDTH = FOX_HEADS * HEAD_DIM
NSA_HEADS = 8
NSA_KV_HEADS = 2
NSA_HPG = NSA_HEADS // NSA_KV_HEADS
NSA_WIDTH = NSA_HEADS * HEAD_DIM
CMP_BLOCK = 64
SEL_BLOCK = 64
TOP_N = 16
N_FORCED_LOCAL = 2
WINDOW = 512
RMS_EPS = 1e-6
NEG = -1e30
SCALE = HEAD_DIM ** -0.5
LOG2E = 1.4426950408889634
SPLIT_SIZES = (FOX_WIDTH, 2 * FOX_WIDTH, FOX_HEADS, FOX_WIDTH,
               NSA_WIDTH, 4 * NSA_KV_HEADS * HEAD_DIM, 2 * NSA_KV_HEADS * HEAD_DIM, 3 * NSA_HEADS, NSA_WIDTH,
               2 * D_MODEL)
SMALL_W = 128
GATE_COL = FOX_HEADS
ALIBI_START = 2.0 ** (-8.0 / NSA_HEADS)

VMEM_LIMIT = 56 * 1024 * 1024

NT_DIMS = (((1,), (1,)), ((), ()))
TN_DIMS = (((0,), (0,)), ((), ()))


def _params(*sem):
    return pltpu.CompilerParams(dimension_semantics=sem, vmem_limit_bytes=VMEM_LIMIT)


def _sigmoid(x):
    return 1.0 / (1.0 + jnp.exp(-x))


def _iota(shape, dim):
    return lax.broadcasted_iota(jnp.int32, shape, dim)


def _div_pow2(x, n):
    assert n & (n - 1) == 0
    return x >> (n.bit_length() - 1)


def _rmsnorm_body(x_ref, g_ref, o_ref):
    x = x_ref[...]
    ms = jnp.mean(x * x, axis=-1, keepdims=True)
    o_ref[...] = (x * lax.rsqrt(ms + RMS_EPS) * g_ref[...]).astype(o_ref.dtype)


def _rmsnorm(x2d, g, out_dtype, tm):
    m, d = x2d.shape
    return pl.pallas_call(
        _rmsnorm_body,
        grid=(m // tm,),
        in_specs=[pl.BlockSpec((tm, d), lambda i: (i, 0)), pl.BlockSpec((1, d), lambda i: (0, 0))],
        out_specs=pl.BlockSpec((tm, d), lambda i: (i, 0)),
        out_shape=jax.ShapeDtypeStruct((m, d), out_dtype),
        compiler_params=_params("parallel"),
        name="rmsnorm",
    )(x2d, g.reshape(1, d))


def _mm_body(h_ref, w_ref, o_ref, *, act):
    acc = jnp.dot(h_ref[...], w_ref[...], preferred_element_type=F32)
    if act == "sigmoid":
        acc = _sigmoid(acc)
    o_ref[...] = acc


def _mm(h, w, act=None, tn=512):
    m, k = h.shape
    n = w.shape[1]
    tm = min(m, 1024)
    tn = min(tn, n)
    return pl.pallas_call(
        functools.partial(_mm_body, act=act),
        grid=(m // tm, n // tn),
        in_specs=[pl.BlockSpec((tm, k), lambda i, j: (i, 0)), pl.BlockSpec((k, tn), lambda i, j: (0, j))],
        out_specs=pl.BlockSpec((tm, tn), lambda i, j: (i, j)),
        out_shape=jax.ShapeDtypeStruct((m, n), F32),
        compiler_params=_params("parallel", "arbitrary"),
        name="proj_" + (act or "lin"),
    )(h, w)


def _mm_small_body(h_ref, w_ref, b_ref, o_ref):
    v = jnp.dot(h_ref[...], w_ref[...], preferred_element_type=F32) + b_ref[...]
    col = _iota(v.shape, 1)
    logsig = jnp.minimum(v, 0.0) - jnp.log1p(jnp.exp(-jnp.abs(v)))
    o_ref[...] = jnp.where(col < FOX_HEADS, logsig, _sigmoid(v))


def _mm_small(h, w, b):
    m, k = h.shape
    tm = min(m, 1024)
    return pl.pallas_call(
        _mm_small_body,
        grid=(m // tm,),
        in_specs=[pl.BlockSpec((tm, k), lambda i: (i, 0)), pl.BlockSpec((k, SMALL_W), lambda i: (0, 0)),
                  pl.BlockSpec((1, SMALL_W), lambda i: (0, 0))],
        out_specs=pl.BlockSpec((tm, SMALL_W), lambda i: (i, 0)),
        out_shape=jax.ShapeDtypeStruct((m, SMALL_W), F32),
        compiler_params=_params("parallel"),
        name="proj_small",
    )(h, w, b)


def _mm_slab_body(h_ref, w_ref, o_ref, *, nslab):
    acc = jnp.dot(h_ref[...], w_ref[...], preferred_element_type=F32)
    for c in range(nslab):
        o_ref[c] = acc[:, c * HEAD_DIM:(c + 1) * HEAD_DIM]


def _mm_slab(h, w):
    m, k = h.shape
    n = w.shape[1]
    nslab = n // HEAD_DIM
    tm = min(m, 1024)
    return pl.pallas_call(
        functools.partial(_mm_slab_body, nslab=nslab),
        grid=(m // tm,),
        in_specs=[pl.BlockSpec((tm, k), lambda i: (i, 0)), pl.BlockSpec((k, n), lambda i: (0, 0))],
        out_specs=pl.BlockSpec((nslab, tm, HEAD_DIM), lambda i: (0, i, 0)),
        out_shape=jax.ShapeDtypeStruct((nslab, m, HEAD_DIM), F32),
        compiler_params=_params("parallel"),
        name="proj_slab",
    )(h, w)


def _compress_body(x_ref, pos_ref, w1_ref, w2_ref, o_ref):
    x = x_ref[0] + pos_ref[0]
    h1 = jnp.dot(x.astype(BF16), w1_ref[0], preferred_element_type=F32)
    a = h1 * _sigmoid(h1)
    o_ref[0] = jnp.dot(a.astype(BF16), w2_ref[0], preferred_element_type=F32)


def _compress(xc, pos, w1, w2):
    nslab, r, kk = xc.shape
    tr = min(r, 256)
    kind = lambda c, i: (c // NSA_KV_HEADS, 0, 0)
    return pl.pallas_call(
        _compress_body,
        grid=(nslab, r // tr),
        in_specs=[pl.BlockSpec((1, tr, kk), lambda c, i: (c, i, 0)),
                  pl.BlockSpec((1, 1, kk), kind),
                  pl.BlockSpec((1, kk, HEAD_DIM), kind),
                  pl.BlockSpec((1, HEAD_DIM, HEAD_DIM), kind)],
        out_specs=pl.BlockSpec((1, tr, HEAD_DIM), lambda c, i: (c, i, 0)),
        out_shape=jax.ShapeDtypeStruct((nslab, r, HEAD_DIM), F32),
        compiler_params=_params("parallel", "parallel"),
        name="nsa_compress",
    )(xc, pos, w1, w2)


def _cumsum_body(lf_ref, ccol_ref, crow_ref, *, t):
    tri = (_iota((128, 128), 0) >= _iota((128, 128), 1)).astype(F32)
    carry = jnp.zeros((1, SMALL_W), F32)
    for ch in range(t // 128):
        rows = slice(ch * 128, (ch + 1) * 128)
        cs = jnp.dot(tri, lf_ref[rows, :], precision=HIGHEST, preferred_element_type=F32) + carry
        ccol_ref[rows, :] = cs
        crow_ref[0, :, rows] = cs.T[0:FOX_HEADS, :]
        carry = cs[127:128, :]


def _cumsum_logf(small, b, t):
    return pl.pallas_call(
        functools.partial(_cumsum_body, t=t),
        grid=(b,),
        in_specs=[pl.BlockSpec((t, SMALL_W), lambda i: (i, 0))],
        out_specs=[pl.BlockSpec((t, SMALL_W), lambda i: (i, 0)),
                   pl.BlockSpec((1, FOX_HEADS, t), lambda i: (i, 0, 0))],
        out_shape=[jax.ShapeDtypeStruct((b * t, SMALL_W), F32), jax.ShapeDtypeStruct((b, FOX_HEADS, t), F32)],
        compiler_params=_params("parallel"),
        name="fox_cumsum",
    )(small)


def _flash_loop(q, k_ref, v_ref, kt_lo, kt_hi, tk, score_fn, select, s_sc, m_sc, l_sc, acc_sc, kt_mid=None):
    m_sc[...] = jnp.full(m_sc.shape, NEG, F32)
    l_sc[...] = jnp.zeros(l_sc.shape, F32)
    acc_sc[...] = jnp.zeros(acc_sc.shape, F32)

    def raw_scores(kt):
        k0 = pl.multiple_of(kt * tk, tk)
        return lax.dot_general(q, k_ref[pl.ds(k0, tk), :].astype(BF16), NT_DIMS, preferred_element_type=F32)

    s_sc[kt_lo & 1] = raw_scores(kt_lo)

    def make_body(masked):
        def body(kt, carry):
            k0 = pl.multiple_of(kt * tk, tk)
            s = s_sc[kt & 1]
            s_sc[(kt + 1) & 1] = raw_scores(jnp.minimum(kt + 1, kt_hi - 1))
            vb = v_ref[pl.ds(k0, tk), :].astype(BF16)
            s, row_c, mask = score_fn(s, kt, k0, masked)
            if mask is not None:
                s = select(mask, s, NEG)
            m_old = m_sc[...]
            m_new = jnp.maximum(m_old, jnp.max(s, axis=-1, keepdims=True) + row_c)
            alpha = jnp.exp2(m_old - m_new)
            p = jnp.exp2(s - (m_new - row_c))
            if mask is not None:
                p = select(mask, p, 0.0)
            l_sc[...] = alpha * l_sc[...] + jnp.sum(p, axis=-1, keepdims=True)
            acc_sc[...] = alpha * acc_sc[...] + jnp.dot(p.astype(BF16), vb, preferred_element_type=F32)
            m_sc[...] = m_new
            return carry
        return body

    if kt_mid is None:
        kt_mid = kt_lo
    else:
        lax.fori_loop(kt_lo, kt_mid, make_body(False), 0)
    lax.fori_loop(kt_mid, kt_hi, make_body(True), 0)
    return acc_sc[...] / jnp.maximum(l_sc[...], 1e-30)


def _fox_body(q_ref, k_ref, v_ref, ccol_ref, crow_ref, o_ref, s_sc, m_sc, l_sc, acc_sc, *, tq, tk):
    h = pl.program_id(1)
    q0 = pl.program_id(2) * tq
    q = (q_ref[...] * (SCALE * LOG2E)).astype(BF16)
    lane = _iota((tq, SMALL_W), 1)
    cq = jnp.sum(jnp.where(lane == h, ccol_ref[...], 0.0), axis=-1, keepdims=True) * LOG2E
    qpos = q0 + _iota((tq, 1), 0)
    kloc = _iota((1, tk), 1)

    def score_fn(s, kt, k0, masked):
        ck = crow_ref[0, 0, kt] * LOG2E
        return s - ck, cq, ((k0 + kloc) <= qpos) if masked else None

    assert tq == tk
    kt_diag = q0 // tk
    o_ref[...] = _flash_loop(q, k_ref, v_ref, 0, kt_diag + 1, tk, score_fn, jnp.where, s_sc, m_sc, l_sc, acc_sc,
                             kt_mid=kt_diag)


def _fox_prompt(fq, fkv, ccol, crow, b, t):
    tq = tk = 512
    nq, nkt = t // tq, t // tk
    crow5 = crow.reshape(b, FOX_HEADS, nkt, 1, tk)
    return pl.pallas_call(
        functools.partial(_fox_body, tq=tq, tk=tk),
        grid=(b, FOX_HEADS, nq),
        in_specs=[pl.BlockSpec((tq, HEAD_DIM), lambda i, h, j: (i * nq + j, h)),
                  pl.BlockSpec((t, HEAD_DIM), lambda i, h, j: (i, h)),
                  pl.BlockSpec((t, HEAD_DIM), lambda i, h, j: (i, FOX_HEADS + h)),
                  pl.BlockSpec((tq, SMALL_W), lambda i, h, j: (i * nq + j, 0)),
                  pl.BlockSpec((1, 1, nkt, 1, tk), lambda i, h, j: (i, h, 0, 0, 0))],
        out_specs=pl.BlockSpec((tq, HEAD_DIM), lambda i, h, j: (i * nq + j, h)),
        out_shape=jax.ShapeDtypeStruct((b * t, FOX_WIDTH), F32),
        scratch_shapes=[pltpu.VMEM((2, tq, tk), F32), pltpu.VMEM((tq, 1), F32), pltpu.VMEM((tq, 1), F32),
                        pltpu.VMEM((tq, HEAD_DIM), F32)],
        compiler_params=_params("parallel", "parallel", "arbitrary"),
        name="fox_prompt",
    )(fq, fkv, fkv, ccol, crow5)


def _masked_softmax(s, mask, axis):
    s = jnp.where(mask, s, NEG)
    m = jnp.max(s, axis=axis, keepdims=True)
    e = jnp.where(mask, jnp.exp(s - m), 0.0)
    return e / jnp.maximum(jnp.sum(e, axis=axis, keepdims=True), 1e-30)


def _group_slope_scale(g):
    scale = jnp.float32(1.0)
    for gg in range(1, NSA_KV_HEADS):
        scale = jnp.where(g == gg, jnp.float32(ALIBI_START ** (NSA_HPG * gg)), scale)
    return scale


def _head_slopes(head_idx, n):
    out = jnp.zeros(head_idx.shape, F32)
    for hh in range(n):
        out = jnp.where(head_idx == hh, jnp.float32(ALIBI_START ** (hh + 1)), out)
    return out


def _nsa_body(q_ref, kc_ref, vc_ref, ks_ref, vs_ref, kw_ref, vw_ref, ocmp_ref, oslc_ref, owin_ref,
              s_sc, m_sc, l_sc, acc_sc, pick_sc, *, tq, tk, n_blk):
    g = pl.program_id(1)
    q0 = pl.program_id(2) * tq
    rows = NSA_HPG * tq
    lg_tq = tq.bit_length() - 1
    qf = q_ref[...]
    q4f = jnp.concatenate([qf[:, j * HEAD_DIM:(j + 1) * HEAD_DIM] for j in range(NSA_HPG)], axis=0)
    q4 = q4f.astype(BF16)
    q4s = (q4f * (SCALE * LOG2E)).astype(BF16)
    gscale = _group_slope_scale(g)

    def unstack(o):
        return jnp.concatenate([o[j * tq:(j + 1) * tq] for j in range(NSA_HPG)], axis=1)

    ri = _iota((rows, 1), 0)
    qpos = q0 + (ri & (tq - 1))
    slope = _head_slopes(ri >> lg_tq, NSA_HPG) * gscale
    kc = kc_ref[0].astype(BF16)
    vc = vc_ref[0].astype(BF16)
    cend = (_iota((1, n_blk), 1) + 1) * CMP_BLOCK - 1
    dc = qpos - cend
    sc = lax.dot_general(q4, kc, NT_DIMS, preferred_element_type=F32) * SCALE - slope * dc.astype(F32)
    p = _masked_softmax(sc, dc >= 0, -1)
    ocmp_ref[...] = unstack(jnp.dot(p.astype(BF16), vc, preferred_element_type=F32))

    ci = _iota((1, rows), 1)
    qpos_t = q0 + (ci & (tq - 1))
    slope_t = _head_slopes(ci >> lg_tq, NSA_HPG) * gscale
    blk = _iota((n_blk, 1), 0)
    dct = qpos_t - ((blk + 1) * CMP_BLOCK - 1)
    sct = lax.dot_general(kc, q4, NT_DIMS, preferred_element_type=F32) * SCALE - slope_t * dct.astype(F32)
    pt = _masked_softmax(sct, dct >= 0, 0)
    imp = pt[:, 0:tq]
    for j in range(1, NSA_HPG):
        imp = imp + pt[:, j * tq:(j + 1) * tq]
    own = _div_pow2(q0 + _iota((1, tq), 1), SEL_BLOCK)
    forced = ((blk <= own) & (blk > own - N_FORCED_LOCAL)) | (blk == 0)
    imp = jnp.where(forced, 1e9, jnp.where(blk > own, -1e9, imp))
    rank = jnp.zeros((n_blk, tq), F32)
    for mm in range(n_blk):
        row = imp[mm:mm + 1, :]
        rank = rank + jnp.where(blk > mm, (row >= imp).astype(F32), (row > imp).astype(F32))
    sel_b = jnp.where(rank < TOP_N, 1.0, 0.0).astype(BF16)

    kloc = _iota((1, tk), 1)
    qpos_q = q0 + _iota((tq, 1), 0)
    slope2 = slope * LOG2E
    bias0 = slope2 * (qpos - kloc).astype(F32)
    e_row = _iota((n_blk, tk), 0)
    e_blk = _div_pow2(_iota((n_blk, tk), 1), SEL_BLOCK)
    for kk in range(n_blk * SEL_BLOCK // tk):
        expand = (e_row == kk * (tk // SEL_BLOCK) + e_blk).astype(BF16)
        pick_sc[kk] = lax.dot_general(sel_b, expand, TN_DIMS, preferred_element_type=F32)

    def alibi(s, k0):
        return s - bias0, slope2 * k0.astype(F32)

    def select(mask_q, x, fill):
        return jnp.where(mask_q[None], x.reshape(NSA_HPG, tq, tk), fill).reshape(rows, tk)

    def slc_score(s, kt, k0, masked):
        mask_q = (pick_sc[kt] > 0.5) & ((k0 + kloc) <= qpos_q)
        return alibi(s, k0) + (mask_q,)

    def win_score(s, kt, k0, masked):
        dist = qpos_q - (k0 + kloc)
        mask_q = (dist >= 0) & (dist < WINDOW)
        return alibi(s, k0) + (mask_q,)

    kt_hi = q0 // tk + 1
    oslc_ref[...] = unstack(_flash_loop(q4s, ks_ref, vs_ref, 0, kt_hi, tk, slc_score, select,
                                        s_sc, m_sc, l_sc, acc_sc))
    kt_lo = jnp.maximum(q0 - (WINDOW - 1), 0) // tk
    owin_ref[...] = unstack(_flash_loop(q4s, kw_ref, vw_ref, kt_lo, kt_hi, tk, win_score, select,
                                        s_sc, m_sc, l_sc, acc_sc))


def _nsa_prompt(nq, cmpkv, nkv, nwkv, b, t):
    tq, tk = 128, 512
    nqt = t // tq
    n_blk = t // CMP_BLOCK
    gw = NSA_HPG * HEAD_DIM
    rows = NSA_HPG * tq
    qmap = lambda i, g, j: (i * nqt + j, g)
    col = lambda c: (lambda i, g, j: (i, c * NSA_KV_HEADS + g))
    slab = lambda c: (lambda i, g, j: (c * NSA_KV_HEADS + g, i, 0))
    out = jax.ShapeDtypeStruct((b * t, NSA_WIDTH), F32)
    return pl.pallas_call(
        functools.partial(_nsa_body, tq=tq, tk=tk, n_blk=n_blk),
        grid=(b, NSA_KV_HEADS, nqt),
        in_specs=[pl.BlockSpec((tq, gw), qmap),
                  pl.BlockSpec((1, n_blk, HEAD_DIM), slab(0)),
                  pl.BlockSpec((1, n_blk, HEAD_DIM), slab(1)),
                  pl.BlockSpec((t, HEAD_DIM), col(2)),
                  pl.BlockSpec((t, HEAD_DIM), col(3)),
                  pl.BlockSpec((t, HEAD_DIM), col(0)),
                  pl.BlockSpec((t, HEAD_DIM), col(1))],
        out_specs=[pl.BlockSpec((tq, gw), qmap)] * 3,
        out_shape=[out, out, out],
        scratch_shapes=[pltpu.VMEM((2, rows, tk), F32), pltpu.VMEM((rows, 1), F32), pltpu.VMEM((rows, 1), F32),
                        pltpu.VMEM((rows, HEAD_DIM), F32), pltpu.VMEM((t // tk, tq, tk), F32)],
        compiler_params=_params("parallel", "parallel", "arbitrary"),
        name="nsa_prompt",
    )(nq, cmpkv, cmpkv, nkv, nkv, nwkv, nwkv)


def _merge_body(x_ref, of_ref, fg_ref, oc_ref, os_ref, ow_ref, ng_ref, sm_ref, mg_ref, wbf_ref, wbn_ref, wo_ref,
                gf_ref, o_ref, *, final):
    fg = fg_ref[...]
    ya = jnp.dot((of_ref[...] * (fg * _sigmoid(fg))).astype(BF16), wbf_ref[...], preferred_element_type=F32)
    sm = sm_ref[...]
    parts = []
    for hd in range(NSA_HEADS):
        cols = slice(hd * HEAD_DIM, (hd + 1) * HEAD_DIM)
        gate = lambda br: sm[:, GATE_COL + br * NSA_HEADS + hd:GATE_COL + br * NSA_HEADS + hd + 1]
        parts.append(gate(0) * oc_ref[:, cols] + gate(1) * os_ref[:, cols] + gate(2) * ow_ref[:, cols])
    ng = ng_ref[...]
    on = jnp.concatenate(parts, axis=1) * (ng * _sigmoid(ng))
    yb = jnp.dot(on.astype(BF16), wbn_ref[...], preferred_element_type=F32)
    mix = mg_ref[:, 0:D_MODEL] * ya + mg_ref[:, D_MODEL:2 * D_MODEL] * yb
    xn = x_ref[...] + jnp.dot(mix.astype(BF16), wo_ref[...], preferred_element_type=F32)
    if final:
        ms = jnp.mean(xn * xn, axis=-1, keepdims=True)
        xn = xn * lax.rsqrt(ms + RMS_EPS) * gf_ref[...]
    o_ref[...] = xn


def _merge(x, o_fox, fgate, o_cmp, o_slc, o_win, ngate, small, mg, w_bf, w_bn, w_o, g_final, final):
    m = x.shape[0]
    tm = min(m, 128)
    row = lambda w: pl.BlockSpec((tm, w), lambda i: (i, 0))
    whole = lambda a: pl.BlockSpec(a.shape, lambda i: (0, 0), pipeline_mode=pl.Buffered(1))
    gf = g_final.reshape(1, D_MODEL)
    return pl.pallas_call(
        functools.partial(_merge_body, final=final),
        grid=(m // tm,),
        in_specs=[row(D_MODEL), row(FOX_WIDTH), row(FOX_WIDTH), row(NSA_WIDTH), row(NSA_WIDTH), row(NSA_WIDTH),
                  row(NSA_WIDTH), row(SMALL_W), row(2 * D_MODEL), whole(w_bf), whole(w_bn), whole(w_o), whole(gf)],
        out_specs=row(D_MODEL),
        out_shape=jax.ShapeDtypeStruct((m, D_MODEL), F32),
        compiler_params=_params("parallel"),
        name="merge_final" if final else "merge",
    )(x, o_fox, fgate, o_cmp, o_slc, o_win, ngate, small, mg, w_bf, w_bn, w_o, gf)


ROWS_PER_TOKEN = 8
assert ROWS_PER_TOKEN == FOX_HEADS == NSA_HEADS == 4 * NSA_KV_HEADS
CMP_CHUNK = 32
assert CMP_BLOCK % CMP_CHUNK == 0


def _rep_rows(col, n):
    return jnp.concatenate([col] * n, axis=0)


def _paged_softmax(val, mask, s_new, n_pages):
    val = jnp.where(mask, val, NEG)
    m_rows = jnp.max(val, axis=-1, keepdims=True)
    m = jnp.maximum(jnp.max(m_rows.reshape(n_pages, ROWS_PER_TOKEN, 1), axis=0), s_new)
    e = jnp.where(mask, jnp.exp(val - _rep_rows(m, n_pages)), 0.0)
    e_new = jnp.exp(s_new - m)
    l_rows = jnp.sum(e, axis=-1, keepdims=True)
    inv = 1.0 / jnp.maximum(jnp.sum(l_rows.reshape(n_pages, ROWS_PER_TOKEN, 1), axis=0) + e_new, 1e-30)
    return e * _rep_rows(inv, n_pages), e_new * inv


def _fox_dec_body(pt_ref, *refs, n_pages, page):
    kv = refs[:n_pages]
    lt = refs[n_pages:2 * n_pages]
    q_ref, kvn_ref, lfn_ref, o_ref = refs[2 * n_pages:]
    hh = ROWS_PER_TOKEN
    rows = n_pages * hh
    lanes = page * hh
    q = q_ref[...]
    qb = q.astype(BF16)
    k_new = kvn_ref[0]
    v_new = kvn_ref[1]
    s_new = jnp.sum(q * k_new, axis=-1, keepdims=True) * SCALE

    lt_all = jnp.concatenate([lt[j][...] for j in range(n_pages)], axis=0)
    later = (_iota((page, page), 0) > _iota((page, page), 1)).astype(F32)
    within = jnp.dot(lt_all, later, precision=HIGHEST, preferred_element_type=F32)
    tot = jnp.broadcast_to(jnp.sum(lt_all, axis=-1, keepdims=True), (rows, page))
    rr = _iota((rows, rows), 0)
    cc = _iota((rows, rows), 1)
    after = (((cc & (hh - 1)) == (rr & (hh - 1))) & (cc > rr)).astype(F32)
    decay = (within + jnp.dot(after, tot, precision=HIGHEST, preferred_element_type=F32)
             + _rep_rows(lfn_ref[...], n_pages))
    hi = decay.astype(BF16)
    r1 = decay - hi.astype(F32)
    mid = r1.astype(BF16)
    lo = (r1 - mid.astype(F32)).astype(BF16)
    spread = (_iota((page, lanes), 0) == _div_pow2(_iota((page, lanes), 1), hh)).astype(BF16)
    d3 = jnp.dot(jnp.concatenate([hi, mid, lo], axis=0), spread, preferred_element_type=F32)
    decay_l = d3[0:rows] + d3[rows:2 * rows] + d3[2 * rows:3 * rows]

    s_all = jnp.concatenate(
        [lax.dot_general(qb, kv[j][:, 0].reshape(lanes, HEAD_DIM).astype(BF16), NT_DIMS, preferred_element_type=F32)
         for j in range(n_pages)], axis=0)
    own_head = (_iota((rows, 1), 0) & (hh - 1)) == (_iota((1, lanes), 1) & (hh - 1))
    p, p_new = _paged_softmax(s_all * SCALE + decay_l, own_head, s_new, n_pages)
    pb = p.astype(BF16)
    acc = p_new * v_new
    for j in range(n_pages):
        vb = kv[j][:, 1].reshape(lanes, HEAD_DIM).astype(BF16)
        acc = acc + jnp.dot(pb[j * hh:(j + 1) * hh], vb, preferred_element_type=F32)
    o_ref[...] = acc


def _fox_decode(cache_kv, cache_lf_t, page_table, layer, fq, fkv, small):
    s, n_pages = page_table.shape
    page = cache_kv.shape[2]
    hh = ROWS_PER_TOKEN
    kv_specs = [pl.BlockSpec((None, None, page, 2, hh, HEAD_DIM), lambda i, pt, j=j: (layer, pt[i, j], 0, 0, 0, 0))
                for j in range(n_pages)]
    lt_specs = [pl.BlockSpec((None, None, hh, page), lambda i, pt, j=j: (layer, pt[i, j], 0, 0))
                for j in range(n_pages)]
    out = pl.pallas_call(
        functools.partial(_fox_dec_body, n_pages=n_pages, page=page),
        grid_spec=pltpu.PrefetchScalarGridSpec(
            num_scalar_prefetch=1,
            grid=(s,),
            in_specs=kv_specs + lt_specs
            + [pl.BlockSpec((None, hh, HEAD_DIM), lambda i, pt: (i, 0, 0)),
               pl.BlockSpec((None, 2, hh, HEAD_DIM), lambda i, pt: (i, 0, 0, 0)),
               pl.BlockSpec((None, hh, 1), lambda i, pt: (i, 0, 0))],
            out_specs=pl.BlockSpec((None, hh, HEAD_DIM), lambda i, pt: (i, 0, 0)),
        ),
        out_shape=jax.ShapeDtypeStruct((s, hh, HEAD_DIM), F32),
        compiler_params=_params("arbitrary"),
        name="fox_decode",
    )(page_table, *([cache_kv] * n_pages), *([cache_lf_t] * n_pages),
      fq.reshape(s, hh, HEAD_DIM), fkv.reshape(s, 2, hh, HEAD_DIM), small[:, 0:hh].reshape(s, hh, 1))
    return out.reshape(s, FOX_WIDTH)


def _nsa_dec_body(pt_ref, *refs, n_pages, page, win_buf):
    pg = refs[:n_pages]
    (win_ref, q_ref, kvn_ref, wn_ref, pos_ref, w1_ref, w2_ref) = refs[n_pages:n_pages + 7]
    ocmp_ref, oslc_ref, owin_ref, wnext_ref, acc_sc, cm_sc = refs[-6:]
    nk = ROWS_PER_TOKEN
    gg_n = NSA_KV_HEADS
    past = n_pages * page
    n_blk = past // CMP_BLOCK
    own = past // SEL_BLOCK
    bpp = page // CMP_BLOCK
    q = q_ref[...]
    qb = q.astype(BF16)
    head = _iota((NSA_HEADS, 1), 0)
    grp = _div_pow2(head, NSA_HPG)
    slope = _head_slopes(head, NSA_HEADS)

    def group_row(a, first):
        out = jnp.zeros((NSA_HEADS, HEAD_DIM), F32)
        for gg in range(gg_n):
            out = jnp.where(grp == gg, jnp.broadcast_to(a[first + gg:first + gg + 1, :], (NSA_HEADS, HEAD_DIM)), out)
        return out

    acc_sc[...] = jnp.zeros(acc_sc.shape, F32)

    def cbody(ic, carry):
        xs = []
        for ii in range(CMP_CHUNK):
            i = ic * CMP_CHUNK + ii
            pos_i = pos_ref[i]
            tiles = []
            for j in range(n_pages):
                for hb in range(bpp):
                    r0 = pl.multiple_of((hb * CMP_BLOCK + i) * nk, nk)
                    tiles.append(pg[j][pl.ds(r0, nk), :] + pos_i)
            xs.append(jnp.concatenate(tiles, axis=0).astype(BF16))
        x = jnp.concatenate(xs, axis=1)
        acc_sc[...] += jnp.dot(x, w1_ref[ic], preferred_element_type=F32)
        return carry

    lax.fori_loop(0, CMP_BLOCK // CMP_CHUNK, cbody, 0)
    krow = _iota((n_blk * nk, 1), 0) & (nk - 1)
    h1 = acc_sc[...]
    h1 = jnp.where(krow < gg_n, h1[:, 0:HEAD_DIM], h1[:, HEAD_DIM:2 * HEAD_DIM])
    c2 = jnp.dot((h1 * _sigmoid(h1)).astype(BF16), w2_ref[...], preferred_element_type=F32)
    cm = jnp.where(krow < gg_n, c2[:, 0:HEAD_DIM], c2[:, HEAD_DIM:2 * HEAD_DIM])
    cm_sc[...] = cm

    def by_group(fn):
        out = fn(0)
        for gg in range(1, gg_n):
            out = jnp.where(grp == gg, fn(gg), out)
        return out

    def scores(rows_fn, first):
        return by_group(lambda gg: lax.dot_general(qb, rows_fn(first + gg), NT_DIMS, preferred_element_type=F32))

    def weighted(pb, rows_fn, first):
        return by_group(lambda gg: jnp.dot(pb, rows_fn(first + gg), preferred_element_type=F32))

    def cmp_rows(k):
        return cm_sc[pl.ds(k, n_blk, stride=nk), :].astype(BF16)

    nb = _iota((1, n_blk), 1)
    dc = past - ((nb + 1) * CMP_BLOCK - 1)
    sc = scores(cmp_rows, 0) * SCALE - slope * dc.astype(F32)
    p = _masked_softmax(sc, dc >= 0, -1)
    ocmp_ref[...] = weighted(p.astype(BF16), cmp_rows, gg_n)

    forced = (nb == 0) | ((nb <= own) & (nb > own - N_FORCED_LOCAL))
    mrow = _iota((n_blk, n_blk), 0)
    ncol = _iota((n_blk, n_blk), 1)
    sel = jnp.zeros((NSA_HEADS, n_blk), F32)
    for gg in range(gg_n):
        imp = jnp.sum(jnp.where(grp == gg, p, 0.0), axis=0, keepdims=True)
        imp = jnp.where(forced, 1e9, imp)
        imp_b = jnp.broadcast_to(imp, (n_blk, n_blk))
        imp_col = jnp.sum(jnp.where(mrow == ncol, imp_b, 0.0), axis=-1, keepdims=True)
        beats = jnp.where(mrow < ncol, (imp_col >= imp_b).astype(F32), (imp_col > imp_b).astype(F32))
        rank = jnp.sum(beats, axis=0, keepdims=True) + jnp.where(forced, 0.0, 1.0)
        sel = jnp.where(grp == gg, jnp.where(rank < TOP_N, 1.0, 0.0), sel)
    expand = (_iota((n_blk, past), 0) == _div_pow2(_iota((n_blk, past), 1), SEL_BLOCK)).astype(BF16)
    sel_tok = jnp.dot(sel.astype(BF16), expand, preferred_element_type=F32) > 0.5

    def page_rows(k):
        return jnp.concatenate([pg[j][pl.ds(k, page, stride=nk), :] for j in range(n_pages)], axis=0).astype(BF16)

    dist = (past - _iota((1, past), 1)).astype(F32)
    val = scores(page_rows, 2 * gg_n) * SCALE - slope * dist
    kvn = kvn_ref[...]
    s_new = jnp.sum(q * group_row(kvn, 2 * gg_n), axis=-1, keepdims=True) * SCALE
    ps, ps_new = _paged_softmax(val, sel_tok, s_new, 1)
    oslc_ref[...] = ps_new * group_row(kvn, 3 * gg_n) + weighted(ps.astype(BF16), page_rows, 3 * gg_n)

    wr = 2 * gg_n

    def win_rows_of(k):
        return win_ref[pl.ds(k, win_buf, stride=wr), :].astype(BF16)

    wdist = win_buf - _iota((1, win_buf), 1)
    sw = scores(win_rows_of, 0) * SCALE - slope * wdist.astype(F32)
    wmask = (wdist < WINDOW) & (past - wdist >= 0)
    wn = wn_ref[...]
    sw_new = jnp.sum(q * group_row(wn, 0), axis=-1, keepdims=True) * SCALE
    pw, pw_new = _paged_softmax(sw, wmask, sw_new, 1)
    owin_ref[...] = pw_new * group_row(wn, gg_n) + weighted(pw.astype(BF16), win_rows_of, gg_n)

    n_rows = win_buf * wr
    shifted = pltpu.roll(win_ref[...], n_rows - wr, axis=0)
    wnext_ref[...] = shifted
    tail = _iota((nk, HEAD_DIM), 0) >= nk - wr
    wnext_ref[n_rows - nk:n_rows, :] = jnp.where(tail, jnp.concatenate([wn] * (nk // wr), axis=0),
                                                 shifted[n_rows - nk:n_rows])


def _nsa_decode(cache_rows, win_rows, page_table, layer, nq, nkv, nwkv, pos8, w1cat, w2cat, win_next):
    s, n_pages = page_table.shape
    depth = win_rows.shape[0]
    nk = ROWS_PER_TOKEN
    page = cache_rows.shape[2] // nk
    wr = 2 * NSA_KV_HEADS
    win_buf = win_rows.shape[2] // wr
    n_blk = n_pages * page // CMP_BLOCK
    pg_specs = [pl.BlockSpec((None, None, page * nk, HEAD_DIM), lambda i, pt, j=j: (layer, pt[i, j], 0, 0))
                for j in range(n_pages)]
    whole = lambda a: pl.BlockSpec(a.shape, lambda i, pt: (0,) * a.ndim, pipeline_mode=pl.Buffered(1))
    head_rows = lambda n: pl.BlockSpec((None, n, HEAD_DIM), lambda i, pt: (i, 0, 0))
    out = jax.ShapeDtypeStruct((s, NSA_HEADS, HEAD_DIM), F32)
    win_spec = pl.BlockSpec((None, None, win_buf * wr, HEAD_DIM), lambda i, pt: (layer, i, 0, 0))
    args = [page_table, *([cache_rows] * n_pages), win_rows,
            nq.reshape(s, NSA_HEADS, HEAD_DIM), nkv.reshape(s, nk, HEAD_DIM), nwkv.reshape(s, wr, HEAD_DIM),
            pos8, w1cat, w2cat]
    in_specs = pg_specs + [win_spec, head_rows(NSA_HEADS), head_rows(nk), head_rows(wr),
                           whole(pos8), whole(w1cat), whole(w2cat)]
    aliases = {}
    if win_next is not None:
        aliases = {len(args): 3}
        args.append(win_next)
        in_specs.append(pl.BlockSpec(memory_space=pl.ANY))
    outs = pl.pallas_call(
        functools.partial(_nsa_dec_body, n_pages=n_pages, page=page, win_buf=win_buf),
        grid_spec=pltpu.PrefetchScalarGridSpec(
            num_scalar_prefetch=1,
            grid=(s,),
            in_specs=in_specs,
            out_specs=[head_rows(NSA_HEADS)] * 3 + [win_spec],
            scratch_shapes=[pltpu.VMEM((n_blk * nk, 2 * HEAD_DIM), F32), pltpu.VMEM((n_blk * nk, HEAD_DIM), F32)],
        ),
        out_shape=[out, out, out, jax.ShapeDtypeStruct((depth, s, win_buf * wr, HEAD_DIM), F32)],
        input_output_aliases=aliases,
        compiler_params=_params("arbitrary"),
        name="nsa_decode",
    )(*args)
    return [o.reshape(s, NSA_WIDTH) for o in outs[:3]], outs[3]


def _decode_compress_weights(w_cmp1_l, w_cmp2_l, cmp_pos_l):
    gg_n = NSA_KV_HEADS
    w1cat = w_cmp1_l.reshape(2, CMP_BLOCK, HEAD_DIM, HEAD_DIM).transpose(1, 2, 0, 3)
    w1cat = w1cat.reshape(CMP_BLOCK // CMP_CHUNK, CMP_CHUNK * HEAD_DIM, 2 * HEAD_DIM).astype(BF16)
    w2cat = jnp.concatenate([w_cmp2_l[0], w_cmp2_l[1]], axis=1).astype(BF16)
    pos = jnp.repeat(cmp_pos_l.transpose(1, 0, 2), gg_n, axis=1)
    pos8 = jnp.concatenate([pos, jnp.zeros((CMP_BLOCK, ROWS_PER_TOKEN - 2 * gg_n, HEAD_DIM), F32)], axis=1)
    return pos8, w1cat, w2cat


def _split_weights(w_in_l, b_f_l):
    off = [int(v) for v in np.cumsum((0,) + SPLIT_SIZES)]
    cols = lambda i: w_in_l[:, off[i]:off[i + 1]]
    names = ("fq", "fkv", "ff", "fgate", "nq", "nkv", "nwkv", "nbg", "ngate", "mg")
    w = {n: cols(i) for i, n in enumerate(names)}
    pad = SMALL_W - FOX_HEADS - 3 * NSA_HEADS
    small = jnp.concatenate([w.pop("ff"), w.pop("nbg"), jnp.zeros((D_MODEL, pad), F32)], axis=1)
    out = {n: v.astype(BF16) for n, v in w.items()}
    out["small"] = small.astype(BF16)
    out["b_small"] = jnp.concatenate([b_f_l, jnp.zeros((SMALL_W - FOX_HEADS,), F32)]).reshape(1, SMALL_W)
    out["cmp"] = out["nkv"][:, 0:2 * NSA_KV_HEADS * HEAD_DIM]
    return out


def _project(x2d, g, w):
    h = _rmsnorm(x2d, g, BF16, min(x2d.shape[0], 512))
    pr = {n: _mm(h, w[n]) for n in ("fq", "fkv", "fgate", "nq", "nkv", "nwkv", "ngate")}
    pr["mg"] = _mm(h, w["mg"], act="sigmoid")
    pr["small"] = _mm_small(h, w["small"], w["b_small"])
    return h, pr


def kernel(x_prompt, x_sample, cache_fox_kv, cache_fox_logf, cache_nsa_kv, cache_nsa_win, page_table,
           norm_g, w_in, b_fox_f, w_cmp1, w_cmp2, cmp_pos, w_branch_fox, w_branch_nsa, w_out, final_norm_g):
    b, t, d = x_prompt.shape
    s, dec_seq, _ = x_sample.shape
    depth, n_phys, page = cache_fox_kv.shape[:3]
    n_pages = page_table.shape[1]
    win_buf = cache_nsa_win.shape[2]
    win_p = min(WINDOW, t)
    assert dec_seq == 1 and d == D_MODEL and t % 512 == 0 and (n_pages * page) % SEL_BLOCK == 0
    assert NSA_KV_HEADS == 2 and win_buf <= n_pages * page and page % CMP_BLOCK == 0

    cache_rows = cache_nsa_kv.reshape(depth, n_phys, page * ROWS_PER_TOKEN, HEAD_DIM)
    win_rows = cache_nsa_win.reshape(depth, s, win_buf * 2 * NSA_KV_HEADS, HEAD_DIM)
    cache_lf_t = jnp.swapaxes(cache_fox_logf, 2, 3)
    page_table = page_table.astype(jnp.int32)

    xp = x_prompt.reshape(b * t, d)
    xs = x_sample.reshape(s, d)
    outs = {k: [] for k in ("fkv_p", "logf_p", "nkv_p", "win_p", "fkv_s", "logf_s", "nkv_s")}
    win_next = None
    for l in range(depth):
        final = l == depth - 1
        w = _split_weights(w_in[l], b_fox_f[l])
        w_bf = w_branch_fox[l].astype(BF16)
        w_bn = w_branch_nsa[l].astype(BF16)
        w_o = w_out[l].astype(BF16)
        w1 = w_cmp1[l].astype(BF16)
        w2 = w_cmp2[l].astype(BF16)
        pos = cmp_pos[l].reshape(2, 1, CMP_BLOCK * HEAD_DIM)

        h, pr = _project(xp, norm_g[l], w)
        ccol, crow = _cumsum_logf(pr["small"], b, t)
        o_fox = _fox_prompt(pr["fq"], pr["fkv"], ccol, crow, b, t)
        xc = _mm_slab(h, w["cmp"]).reshape(2 * NSA_KV_HEADS, b * (t // CMP_BLOCK), CMP_BLOCK * HEAD_DIM)
        cmpkv = _compress(xc, pos, w1, w2)
        o_cmp, o_slc, o_win = _nsa_prompt(pr["nq"], cmpkv, pr["nkv"], pr["nwkv"], b, t)
        xp = _merge(xp, o_fox, pr["fgate"], o_cmp, o_slc, o_win, pr["ngate"], pr["small"], pr["mg"],
                    w_bf, w_bn, w_o, final_norm_g, final)
        outs["fkv_p"].append(pr["fkv"].reshape(b, t, 2, FOX_HEADS, HEAD_DIM))
        outs["logf_p"].append(pr["small"][:, 0:FOX_HEADS].reshape(b, t, FOX_HEADS))
        outs["nkv_p"].append(pr["nkv"].reshape(b, t, 4, NSA_KV_HEADS, HEAD_DIM))
        outs["win_p"].append(pr["nwkv"].reshape(b, t, 2, NSA_KV_HEADS, HEAD_DIM)[:, t - win_p:])

        _, ps = _project(xs, norm_g[l], w)
        o_fox = _fox_decode(cache_fox_kv, cache_lf_t, page_table, l, ps["fq"], ps["fkv"], ps["small"])
        (o_cmp, o_slc, o_win), win_next = _nsa_decode(
            cache_rows, win_rows, page_table, l, ps["nq"], ps["nkv"], ps["nwkv"],
            *_decode_compress_weights(w_cmp1[l], w_cmp2[l], cmp_pos[l]), win_next)
        xs = _merge(xs, o_fox, ps["fgate"], o_cmp, o_slc, o_win, ps["ngate"], ps["small"], ps["mg"],
                    w_bf, w_bn, w_o, final_norm_g, final)
        outs["fkv_s"].append(ps["fkv"].reshape(s, 1, 2, FOX_HEADS, HEAD_DIM))
        outs["logf_s"].append(ps["small"][:, 0:FOX_HEADS].reshape(s, 1, FOX_HEADS))
        outs["nkv_s"].append(ps["nkv"].reshape(s, 1, 4, NSA_KV_HEADS, HEAD_DIM))

    st = {k: jnp.stack(v, axis=0) for k, v in outs.items()}
    win_s = win_next.reshape(cache_nsa_win.shape)
    return (xp.reshape(b, t, d), xs.reshape(s, 1, d), st["fkv_p"], st["logf_p"], st["nkv_p"], st["win_p"],
            st["fkv_s"], st["logf_s"], st["nkv_s"], win_s)
```
